```python
import math
import jax, jax.numpy as jnp
from jax import lax
import numpy as np

D_MODEL = 2048
BATCH = 4
SEQ = 4096
DEPTH = 1

HEAD_DIM = 64
SWA_Q_HEADS = 16
SWA_KV_HEADS = 4
SWA_WINDOW = 128
MOBA_HEADS = 16
MOBA_BLOCK = 256
MOBA_TOPK = 3
MOBA_Q_CHUNK = 16
N_HEADS_TOTAL = SWA_Q_HEADS + MOBA_HEADS
MIX_WIDTH = N_HEADS_TOTAL * HEAD_DIM
IN_SPLITS = (SWA_Q_HEADS * HEAD_DIM, SWA_KV_HEADS * HEAD_DIM, SWA_KV_HEADS * HEAD_DIM,
             MOBA_HEADS * HEAD_DIM, MOBA_HEADS * HEAD_DIM, MOBA_HEADS * HEAD_DIM)
IN_WIDTH = sum(IN_SPLITS)
T5_BUCKETS = 32
T5_MAX_DISTANCE = 128
N_EXPERTS = 32
TOP_K = 4
D_FF = D_MODEL
SWIGLU_LIMIT = 7.0
SWIGLU_ALPHA = 1.702
MOE_BLOCK = 256
NORM_EPS = 1e-5
ATTN_SCALE = HEAD_DIM ** -0.5

kernel_name = "hybrid_swa_sink_moba_moe_layer"


def rms_norm(x, g):
    xf = x.astype(jnp.float32)
    y = xf * lax.rsqrt(jnp.mean(xf * xf, axis=-1, keepdims=True) + NORM_EPS) * g.astype(jnp.float32)
    return y.astype(x.dtype)


def t5_bucket(dist):
    max_exact = T5_BUCKETS // 2
    d = jnp.maximum(dist, 0)
    df = jnp.maximum(d, 1).astype(jnp.float32)
    large = max_exact + (jnp.log(df / max_exact) / math.log(T5_MAX_DISTANCE / max_exact)
                         * (T5_BUCKETS - max_exact)).astype(jnp.int32)
    large = jnp.minimum(large, T5_BUCKETS - 1)
    return jnp.where(d < max_exact, d, large)


def swa_attention(q, k, v, sinks, rel_tab):
    B, S, H, Dh = q.shape
    KV = k.shape[2]
    G = H // KV
    W = SWA_WINDOW
    nb = S // W

    def band(t):
        prev = jnp.pad(t, ((0, 0), (W, 0), (0, 0), (0, 0)))[:, :S]
        return jnp.concatenate([prev.reshape(B, nb, W, KV, Dh), t.reshape(B, nb, W, KV, Dh)], axis=2)

    kb, vb = band(k), band(v)
    qb = q.reshape(B, nb, W, KV, G, Dh)
    s = jnp.einsum('bnqkgd,bnckd->bkgnqc', qb, kb).astype(jnp.float32) * ATTN_SCALE
    r = jnp.arange(W)[:, None]
    c = jnp.arange(2 * W)[None, :]
    dist = r + W - c
    bias = rel_tab[t5_bucket(dist)]
    bias = jnp.moveaxis(bias, -1, 0).reshape(KV, G, 1, W, 2 * W).astype(jnp.float32)
    kpos = jnp.arange(nb)[:, None, None] * W - W + c[None]
    mask = ((dist >= 0) & (dist < W))[None] & (kpos >= 0)
    s = jnp.where(mask, s + bias, -jnp.inf)
    sink = jnp.broadcast_to(sinks.astype(jnp.float32).reshape(1, KV, G, 1, 1, 1), (B, KV, G, nb, W, 1))
    p = jax.nn.softmax(jnp.concatenate([s, sink], axis=-1), axis=-1)[..., :-1]
    o = jnp.einsum('bkgnqc,bnckd->bnqkgd', p.astype(v.dtype), vb)
    return o.reshape(B, S, H * Dh)


def moba_attention(q, k, v, rel_tab):
    B, S, H, Dh = q.shape
    L = MOBA_BLOCK
    nblk = -(-S // L)
    Sp = nblk * L
    ksel = min(MOBA_TOPK, nblk)
    Qc = MOBA_Q_CHUNK
    nch = S // Qc
    q = q.transpose(0, 2, 1, 3)
    kp = jnp.pad(k.transpose(0, 2, 1, 3), ((0, 0), (0, 0), (0, Sp - S), (0, 0)))
    vp = jnp.pad(v.transpose(0, 2, 1, 3), ((0, 0), (0, 0), (0, Sp - S), (0, 0)))
    kblk = kp.reshape(B, H, nblk, L, Dh)
    vblk = vp.reshape(B, H, nblk, L, Dh)
    kmean = jnp.mean(kblk.astype(jnp.float32), axis=3)
    gate = jnp.einsum('bhsd,bhnd->bhsn', q.astype(jnp.float32), kmean)
    cur = jnp.arange(S) // L
    past = jnp.arange(nblk)[None, :] < cur[:, None]
    gate = jnp.where(past, gate, -jnp.inf)
    _, sel = lax.top_k(gate, ksel)
    rel_t = rel_tab.T.astype(jnp.float32)
    gather_blocks = jax.vmap(jax.vmap(lambda blk, idx: blk[idx]))
    q_ch = q.reshape(B, H, nch, Qc, Dh).transpose(2, 0, 1, 3, 4)
    sel_ch = sel.reshape(B, H, nch, Qc, ksel).transpose(2, 0, 1, 3, 4)
    h_idx = jnp.arange(H)[None, :, None, None, None]

    def body(args):
        qc, selc, ci = args
        t = ci * Qc + jnp.arange(Qc)
        own_start = (ci * Qc) // L * L
        own_blk = own_start // L
        kg = gather_blocks(kblk, selc)
        vg = gather_blocks(vblk, selc)
        s_sel = jnp.einsum('bhqd,bhqjcd->bhqjc', qc, kg).astype(jnp.float32) * ATTN_SCALE
        kpos_sel = selc[..., None] * L + jnp.arange(L)
        dist_sel = t[None, None, :, None, None] - kpos_sel
        bias_sel = rel_t[h_idx, t5_bucket(dist_sel)]
        valid = (selc < own_blk)[..., None]
        s_sel = jnp.where(valid, s_sel + bias_sel, -jnp.inf).reshape(B, H, Qc, ksel * L)
        k_own = lax.dynamic_slice_in_dim(kp, own_start, L, axis=2)
        v_own = lax.dynamic_slice_in_dim(vp, own_start, L, axis=2)
        s_own = jnp.einsum('bhqd,bhcd->bhqc', qc, k_own).astype(jnp.float32) * ATTN_SCALE
        dist_own = t[:, None] - (own_start + jnp.arange(L))[None, :]
        bias_own = rel_t[:, t5_bucket(dist_own)]
        s_own = jnp.where(dist_own >= 0, s_own + bias_own, -jnp.inf)
        p = jax.nn.softmax(jnp.concatenate([s_sel, s_own], axis=-1), axis=-1).astype(vp.dtype)
        p_sel = p[..., :ksel * L].reshape(B, H, Qc, ksel, L)
        p_own = p[..., ksel * L:]
        o = jnp.einsum('bhqjc,bhqjcd->bhqd', p_sel, vg) + jnp.einsum('bhqc,bhcd->bhqd', p_own, v_own)
        return o.astype(qc.dtype)

    out = lax.map(body, (q_ch, sel_ch, jnp.arange(nch)))
    out = out.transpose(1, 0, 3, 2, 4).reshape(B, S, H * Dh)
    return out


def moe_ffn(h, w_router, b_router, w_gate_up, b_gate_up, w_down, b_down):
    B, S, D = h.shape
    T = B * S
    hf = h.reshape(T, D)
    logits = (hf @ w_router).astype(jnp.float32) + b_router.astype(jnp.float32)
    top_val, top_idx = lax.top_k(logits, TOP_K)
    gates = jax.nn.softmax(top_val, axis=-1)
    A = T * TOP_K
    e_flat = top_idx.reshape(A)
    tok_flat = jnp.arange(A, dtype=jnp.int32) // TOP_K
    g_flat = gates.reshape(A)
    order = jnp.argsort(e_flat, stable=True)
    e_sorted = e_flat[order]
    tok_sorted = tok_flat[order]
    g_sorted = g_flat[order]
    counts = jnp.bincount(e_flat, length=N_EXPERTS)
    starts = jnp.cumsum(counts) - counts
    padded = (counts + MOE_BLOCK - 1) // MOE_BLOCK * MOE_BLOCK
    pad_ends = jnp.cumsum(padded)
    pad_starts = pad_ends - padded
    dest = pad_starts[e_sorted] + (jnp.arange(A) - starts[e_sorted])
    n_blocks = -(-(A + N_EXPERTS * (MOE_BLOCK - 1)) // MOE_BLOCK)
    P = n_blocks * MOE_BLOCK
    buf_tok = jnp.zeros((P,), jnp.int32).at[dest].set(tok_sorted)
    buf_gate = jnp.zeros((P,), jnp.float32).at[dest].set(g_sorted)
    block_expert = jnp.minimum(jnp.searchsorted(pad_ends, jnp.arange(n_blocks) * MOE_BLOCK, side='right'),
                               N_EXPERTS - 1)

    def expert_block(args):
        tok, g, e = args
        xs = hf[tok]
        gu = xs @ w_gate_up[e] + b_gate_up[e]
        x_glu = jnp.minimum(gu[:, ::2], SWIGLU_LIMIT)
        x_lin = jnp.clip(gu[:, 1::2], -SWIGLU_LIMIT, SWIGLU_LIMIT)
        act = x_glu * jax.nn.sigmoid(SWIGLU_ALPHA * x_glu) * (x_lin + 1)
        y = act @ w_down[e] + b_down[e]
        return y.astype(jnp.float32) * g[:, None]

    ys = lax.map(expert_block, (buf_tok.reshape(n_blocks, MOE_BLOCK), buf_gate.reshape(n_blocks, MOE_BLOCK),
                                block_expert))
    out = jnp.zeros((T, D), jnp.float32).at[buf_tok].add(ys.reshape(P, D))
    return out.astype(h.dtype).reshape(B, S, D)


def setup_inputs(seed: int = 0) -> dict:
    key = jax.random.key(seed)
    ks = jax.random.split(key, 20)
    f32 = jnp.float32
    nrm = lambda k, shape, s: jax.random.normal(k, shape, f32) * s
    return {
        "x": nrm(ks[0], (BATCH, SEQ, D_MODEL), 1.0),
        "attn_norm_g": 1.0 + nrm(ks[1], (DEPTH, D_MODEL), 0.1),
        "w_in": nrm(ks[2], (DEPTH, D_MODEL, IN_WIDTH), D_MODEL ** -0.5),
        "swa_q_gain": 1.0 + nrm(ks[3], (DEPTH, HEAD_DIM), 0.1),
        "swa_k_gain": 1.0 + nrm(ks[4], (DEPTH, HEAD_DIM), 0.1),
        "swa_sinks": nrm(ks[5], (DEPTH, SWA_Q_HEADS), 1.0),
        "moba_q_gain": 1.0 + nrm(ks[6], (DEPTH, HEAD_DIM), 0.1),
        "moba_k_gain": 1.0 + nrm(ks[7], (DEPTH, HEAD_DIM), 0.1),
        "rel_bias": nrm(ks[8], (T5_BUCKETS, N_HEADS_TOTAL), 0.3),
        "w_out": nrm(ks[9], (DEPTH, MIX_WIDTH, D_MODEL), MIX_WIDTH ** -0.5),
        "ffn_norm_g": 1.0 + nrm(ks[10], (DEPTH, D_MODEL), 0.1),
        "w_router": nrm(ks[11], (DEPTH, D_MODEL, N_EXPERTS), D_MODEL ** -0.5),
        "b_router": nrm(ks[12], (DEPTH, N_EXPERTS), 0.01),
        "w_gate_up": nrm(ks[13], (DEPTH, N_EXPERTS, D_MODEL, 2 * D_FF), D_MODEL ** -0.5),
        "b_gate_up": nrm(ks[14], (DEPTH, N_EXPERTS, 2 * D_FF), 0.01),
        "w_down": nrm(ks[15], (DEPTH, N_EXPERTS, D_FF, D_MODEL), D_FF ** -0.5),
        "b_down": nrm(ks[16], (DEPTH, N_EXPERTS, D_MODEL), 0.01),
    }


def reference(x, attn_norm_g, w_in, swa_q_gain, swa_k_gain, swa_sinks, moba_q_gain, moba_k_gain,
              rel_bias, w_out, ffn_norm_g, w_router, b_router, w_gate_up, b_gate_up, w_down, b_down):
    B, S, D = x.shape
    split_at = [int(v) for v in np.cumsum(IN_SPLITS)[:-1]]
    rel_a = rel_bias[:, :SWA_Q_HEADS]
    rel_b = rel_bias[:, SWA_Q_HEADS:]
    for layer in range(DEPTH):
        h = rms_norm(x, attn_norm_g[layer])
        proj = h @ w_in[layer]
        qa, ka, va, qb, kb, vb = jnp.split(proj, split_at, axis=-1)
        qa = rms_norm(qa.reshape(B, S, SWA_Q_HEADS, HEAD_DIM), swa_q_gain[layer])
        ka = rms_norm(ka.reshape(B, S, SWA_KV_HEADS, HEAD_DIM), swa_k_gain[layer])
        va = va.reshape(B, S, SWA_KV_HEADS, HEAD_DIM)
        qb = rms_norm(qb.reshape(B, S, MOBA_HEADS, HEAD_DIM), moba_q_gain[layer])
        kb = rms_norm(kb.reshape(B, S, MOBA_HEADS, HEAD_DIM), moba_k_gain[layer])
        vb = vb.reshape(B, S, MOBA_HEADS, HEAD_DIM)
        oa = swa_attention(qa, ka, va, swa_sinks[layer], rel_a)
        ob = moba_attention(qb, kb, vb, rel_b)
        x = x + jnp.concatenate([oa, ob], axis=-1) @ w_out[layer]
        h2 = rms_norm(x, ffn_norm_g[layer])
        x = x + moe_ffn(h2, w_router[layer], b_router[layer], w_gate_up[layer], b_gate_up[layer],
                        w_down[layer], b_down[layer])
    return x
```

```python
import functools
import math

import numpy as np
import jax
import jax.numpy as jnp
from jax import lax
from jax.experimental import pallas as pl
from jax.experimental.pallas import tpu as pltpu

F32 = jnp.float32
BF16 = jnp.bfloat16
I32 = jnp.int32

HEAD_DIM = 64
SWA_Q_HEADS = 16
SWA_KV_HEADS = 4
SWA_WINDOW = 128
MOBA_HEADS = 16
MOBA_BLOCK = 256
MOBA_TOPK = 3
T5_BUCKETS = 32
T5_MAX_DISTANCE = 128
N_EXPERTS = 32
TOP_K = 4
SWIGLU_LIMIT = 7.0
SWIGLU_ALPHA = 1.702
MOE_BLOCK = 256
NORM_EPS = 1e-5
ATTN_SCALE = HEAD_DIM ** -0.5

LANES = 128
NEG = -1e30
VMEM_LIMIT = 56 * 1024 * 1024
_GATE_SLOTS = 16
_HEADS_PER_MASK_BLOCK = LANES // _GATE_SLOTS

_QA, _KA, _VA, _QB, _KB, _VB = 0, 8, 10, 12, 20, 28


def _dot(a, b):
    return jnp.dot(a, b, preferred_element_type=F32)


def _dot_nt(a, b):
    return lax.dot_general(a, b, (((1,), (1,)), ((), ())), preferred_element_type=F32)


def _split_bf16(x):
    hi = x.astype(BF16)
    lo = (x - hi.astype(F32)).astype(BF16)
    return hi, lo


def _head_lane_masks(rows):
    lane = lax.broadcasted_iota(I32, (rows, LANES), 1)
    lo = jnp.where(lane < HEAD_DIM, 1.0, 0.0).astype(BF16)
    return lo, (1.0 - lo.astype(F32)).astype(BF16)


def _params(sem):
    return pltpu.CompilerParams(dimension_semantics=sem, vmem_limit_bytes=VMEM_LIMIT)


def _inproj_kernel(x_ref, g_ref, w_ref, gain_ref, flag_ref, bd_ref, o_ref, h_scr):
    @pl.when(pl.program_id(1) == 0)
    def _():
        x = x_ref[...]
        ms = jnp.mean(x * x, axis=-1, keepdims=True)
        h_scr[...] = (x * lax.rsqrt(ms + NORM_EPS) * g_ref[...]).astype(BF16)

    y = _dot(h_scr[...], w_ref[...])
    bd = bd_ref[...]
    for c in range(y.shape[1] // LANES):
        yc = y[:, c * LANES:(c + 1) * LANES]
        hi, lo = _split_bf16(yc * yc)
        ssum = _dot(hi, bd) + _dot(lo, bd)
        r = lax.rsqrt(ssum * (1.0 / HEAD_DIM) + NORM_EPS)
        sl = slice(c * LANES, (c + 1) * LANES)
        scale = jnp.where(flag_ref[:, sl] > 0.5, r, 1.0) * gain_ref[:, sl]
        o_ref[:, sl] = (yc * scale).astype(BF16)


def _inproj(x2, g, w_bf, gain_col, flag_col, tm=512, tn=1536):
    T, D = x2.shape
    N = w_bf.shape[1]
    blk = np.kron(np.eye(LANES // HEAD_DIM), np.ones((HEAD_DIM, HEAD_DIM))).astype(np.float32)
    bd = jnp.asarray(blk, dtype=BF16)
    return pl.pallas_call(
        _inproj_kernel,
        out_shape=jax.ShapeDtypeStruct((T, N), BF16),
        grid=(T // tm, N // tn),
        in_specs=[
            pl.BlockSpec((tm, D), lambda i, j: (i, 0)),
            pl.BlockSpec((1, D), lambda i, j: (0, 0)),
            pl.BlockSpec((D, tn), lambda i, j: (0, j)),
            pl.BlockSpec((1, tn), lambda i, j: (0, j)),
            pl.BlockSpec((1, tn), lambda i, j: (0, j)),
            pl.BlockSpec((LANES, LANES), lambda i, j: (0, 0)),
        ],
        out_specs=pl.BlockSpec((tm, tn), lambda i, j: (i, j)),
        scratch_shapes=[pltpu.VMEM((tm, D), BF16)],
        compiler_params=_params(("arbitrary", "arbitrary")),
        name="inproj",
    )(x2, g, w_bf, gain_col, flag_col, bd)


def _kmean_kernel(k_ref, o_ref):
    nblk = o_ref.shape[1]
    for j in range(nblk):
        kj = k_ref[0, j * MOBA_BLOCK:(j + 1) * MOBA_BLOCK, :].astype(F32)
        o_ref[0, j:j + 1, :] = jnp.sum(kj, axis=0, keepdims=True) * (1.0 / MOBA_BLOCK)


def _kmean(proj3):
    B, S, _ = proj3.shape
    nblk = S // MOBA_BLOCK
    W = MOBA_HEADS * HEAD_DIM
    return pl.pallas_call(
        _kmean_kernel,
        out_shape=jax.ShapeDtypeStruct((B, nblk, W), F32),
        grid=(B, W // 512),
        in_specs=[pl.BlockSpec((1, S, 512), lambda b, w: (b, 0, _KB * LANES // 512 + w))],
        out_specs=pl.BlockSpec((1, nblk, 512), lambda b, w: (b, 0, w)),
        compiler_params=_params(("arbitrary", "arbitrary")),
        name="kmean",
    )(proj3)


def _select_kernel(qlo_ref, qhi_ref, kmh_ref, kml_ref, o_ref):
    c = pl.program_id(1)
    half_k = qlo_ref.shape[2]
    qlo = qlo_ref[0]
    qhi = qhi_ref[0]
    g = (_dot(qlo, kmh_ref[0, :half_k, :]) + _dot(qhi, kmh_ref[0, half_k:, :])
         + _dot(qlo, kml_ref[0, :half_k, :]) + _dot(qhi, kml_ref[0, half_k:, :]))
    nb = _GATE_SLOTS
    lane = lax.broadcasted_iota(I32, (g.shape[0], LANES), 1)
    j = lane & (nb - 1)
    for half in range(g.shape[1] // LANES):
        gh = g[:, half * LANES:(half + 1) * LANES]
        cnt = jnp.zeros(gh.shape, F32)
        for d in range(1, nb):
            a = pltpu.roll(gh, LANES - d, 1)
            b = pltpu.roll(gh, nb - d, 1)
            wrap = (j + d) >= nb
            partner = jnp.where(wrap, b, a)
            jp = jnp.where(wrap, j + (d - nb), j + d)
            ahead = jnp.where(jp < j, jnp.where(partner >= gh, 1.0, 0.0), jnp.where(partner > gh, 1.0, 0.0))
            cnt = cnt + jnp.where(jp < c, ahead, 0.0)
        sel = jnp.where(j < c, cnt, 1e9) < (MOBA_TOPK - 0.5)
        o_ref[0, :, half * LANES:(half + 1) * LANES] = jnp.where(sel, 0.0, NEG).astype(BF16)


def _select(proj3, kmbd_hi, kmbd_lo):
    B, S, _ = proj3.shape
    nq = S // MOBA_BLOCK
    KW = MOBA_HEADS * HEAD_DIM
    NW = kmbd_hi.shape[2]
    qb0 = _QB * LANES // 512
    return pl.pallas_call(
        _select_kernel,
        out_shape=jax.ShapeDtypeStruct((B, S, NW), BF16),
        grid=(B, nq),
        in_specs=[
            pl.BlockSpec((1, MOBA_BLOCK, 512), lambda b, c: (b, c, qb0)),
            pl.BlockSpec((1, MOBA_BLOCK, 512), lambda b, c: (b, c, qb0 + 1)),
            pl.BlockSpec((1, KW, NW), lambda b, c: (b, 0, 0)),
            pl.BlockSpec((1, KW, NW), lambda b, c: (b, 0, 0)),
        ],
        out_specs=pl.BlockSpec((1, MOBA_BLOCK, NW), lambda b, c: (b, c, 0)),
        compiler_params=_params(("arbitrary", "arbitrary")),
        name="moba_select",
    )(proj3, proj3, kmbd_hi, kmbd_lo)


def _swa_kernel(q_ref, kc_ref, kp_ref, vc_ref, vp_ref, bias_ref, sink_ref, o_ref, k_scr, v_scr):
    W = SWA_WINDOW
    tq = q_ref.shape[1]
    first = pl.program_id(2) == 0
    k_scr[0:W, :] = kp_ref[0]
    k_scr[W:, :] = kc_ref[0]
    v_scr[0:W, :] = vp_ref[0]
    v_scr[W:, :] = vc_ref[0]
    lane = lax.broadcasted_iota(I32, (W, LANES), 1)
    keep_lo, keep_hi = _head_lane_masks(W)
    col = lax.broadcasted_iota(I32, (1, 2 * W), 1)
    nopast = jnp.where(jnp.logical_and(first, col < W), NEG, 0.0)
    for sb in range(tq // W):
        k2 = k_scr[sb * W:sb * W + 2 * W, :]
        v2 = v_scr[sb * W:sb * W + 2 * W, :]
        for c in range(q_ref.shape[2] // LANES):
            qc = q_ref[0, sb * W:(sb + 1) * W, c * LANES:(c + 1) * LANES]
            qab = jnp.concatenate([qc * keep_lo, qc * keep_hi], axis=0)
            s = _dot_nt(qab, k2) + bias_ref[0, c]
            if sb == 0:
                s = s + nopast
            sink = sink_ref[0, c]
            m = jnp.maximum(jnp.max(s, axis=-1, keepdims=True), sink)
            p = jnp.exp(s - m)
            l = jnp.sum(p, axis=-1, keepdims=True) + jnp.exp(sink - m)
            o = _dot(p.astype(BF16), v2) / l
            o_ref[0, sb * W:(sb + 1) * W, c * LANES:(c + 1) * LANES] = jnp.where(
                lane < HEAD_DIM, o[:W], o[W:]).astype(BF16)


def _swa(proj3, bias_swa, sink_swa, tq=512):
    B, S, _ = proj3.shape
    W = SWA_WINDOW
    r = tq // W
    npair = SWA_KV_HEADS // 2
    qw = SWA_Q_HEADS * HEAD_DIM // npair
    nc = qw // LANES
    return pl.pallas_call(
        _swa_kernel,
        out_shape=jax.ShapeDtypeStruct((B, S, SWA_Q_HEADS * HEAD_DIM), BF16),
        grid=(B, npair, S // tq),
        in_specs=[
            pl.BlockSpec((1, tq, qw), lambda b, p, i: (b, i, p)),
            pl.BlockSpec((1, tq, LANES), lambda b, p, i: (b, i, _KA + p)),
            pl.BlockSpec((1, W, LANES), lambda b, p, i: (b, jnp.maximum(i * r - 1, 0), _KA + p)),
            pl.BlockSpec((1, tq, LANES), lambda b, p, i: (b, i, _VA + p)),
            pl.BlockSpec((1, W, LANES), lambda b, p, i: (b, jnp.maximum(i * r - 1, 0), _VA + p)),
            pl.BlockSpec((1, nc, 2 * W, 2 * W), lambda b, p, i: (p, 0, 0, 0)),
            pl.BlockSpec((1, nc, 2 * W, 1), lambda b, p, i: (p, 0, 0, 0)),
        ],
        out_specs=pl.BlockSpec((1, tq, qw), lambda b, p, i: (b, i, p)),
        scratch_shapes=[pltpu.VMEM((tq + W, LANES), BF16), pltpu.VMEM((tq + W, LANES), BF16)],
        compiler_params=_params(("arbitrary", "arbitrary", "arbitrary")),
        name="swa_attn",
    )(proj3, proj3, proj3, proj3, proj3, bias_swa, sink_swa)


def _moba_kernel(q_ref, k_ref, v_ref, m_ref, bo_ref, ba_ref, far_ref, o_ref, m_scr, l_scr, acc_scr):
    L = MOBA_BLOCK
    p = pl.program_id(1)
    c = pl.program_id(2)
    lane = lax.broadcasted_iota(I32, (L, LANES), 1)
    q2 = q_ref[0]
    keep_lo, keep_hi = _head_lane_masks(L)
    qm = [q2 * keep_lo, q2 * keep_hi]
    mblk = m_ref[0]

    def rows(j):
        return pl.ds(pl.multiple_of(j * L, L), L)

    kc = k_ref[0, rows(c), :]
    vc = v_ref[0, rows(c), :]
    for hh in range(2):
        s = _dot_nt(qm[hh], kc) + bo_ref[hh]
        m = jnp.max(s, axis=-1, keepdims=True)
        pr = jnp.exp(s - m)
        m_scr[hh] = m
        l_scr[hh] = jnp.sum(pr, axis=-1, keepdims=True)
        acc_scr[hh] = _dot(pr.astype(BF16), vc)

    def past_tile(j, tile_bias, const_bias):
        kj = k_ref[0, rows(j), :]
        vj = v_ref[0, rows(j), :]
        for hh in range(2):
            slot = ((2 * p + hh) % _HEADS_PER_MASK_BLOCK) * _GATE_SLOTS + j
            onehot = jnp.where(lane == slot, 1.0, 0.0).astype(BF16)
            s = _dot_nt(jnp.concatenate([qm[hh], mblk], axis=1),
                        jnp.concatenate([kj, onehot], axis=1))
            if tile_bias is not None:
                s = s + tile_bias[hh]
            cb = const_bias[hh] if const_bias is not None else 0.0
            m_old = m_scr[hh]
            m_new = jnp.maximum(m_old, jnp.max(s, axis=-1, keepdims=True) + cb)
            pr = jnp.exp(s - (m_new - cb))
            alpha = jnp.exp(m_old - m_new)
            m_scr[hh] = m_new
            l_scr[hh] = l_scr[hh] * alpha + jnp.sum(pr, axis=-1, keepdims=True)
            acc_scr[hh] = acc_scr[hh] * alpha + _dot(pr.astype(BF16), vj)

    def far_body(j, carry):
        past_tile(j, None, [far_ref[0, :, 0:1], far_ref[1, :, 0:1]])
        return carry

    lax.fori_loop(0, jnp.maximum(c - 1, 0), far_body, 0)

    @pl.when(c >= 1)
    def _():
        past_tile(c - 1, ba_ref, None)

    o_ref[0] = jnp.where(lane < HEAD_DIM, acc_scr[0] / l_scr[0], acc_scr[1] / l_scr[1]).astype(BF16)


def _moba(proj3, mask, bias_own, bias_adj, far):
    B, S, _ = proj3.shape
    L = MOBA_BLOCK
    npair = MOBA_HEADS // 2
    return pl.pallas_call(
        _moba_kernel,
        out_shape=jax.ShapeDtypeStruct((B, S, MOBA_HEADS * HEAD_DIM), BF16),
        grid=(B, npair, S // L),
        in_specs=[
            pl.BlockSpec((1, L, LANES), lambda b, p, c: (b, c, _QB + p)),
            pl.BlockSpec((1, S, LANES), lambda b, p, c: (b, 0, _KB + p)),
            pl.BlockSpec((1, S, LANES), lambda b, p, c: (b, 0, _VB + p)),
            pl.BlockSpec((1, L, LANES), lambda b, p, c: (b, c, 2 * p // _HEADS_PER_MASK_BLOCK)),
            pl.BlockSpec((2, L, L), lambda b, p, c: (p, 0, 0)),
            pl.BlockSpec((2, L, L), lambda b, p, c: (p, 0, 0)),
            pl.BlockSpec((2, 1, LANES), lambda b, p, c: (p, 0, 0)),
        ],
        out_specs=pl.BlockSpec((1, L, LANES), lambda b, p, c: (b, c, p)),
        scratch_shapes=[pltpu.VMEM((2, L, 1), F32), pltpu.VMEM((2, L, 1), F32),
                        pltpu.VMEM((2, L, LANES), F32)],
        compiler_params=_params(("arbitrary", "arbitrary", "arbitrary")),
        name="moba_attn",
    )(proj3, proj3, proj3, mask, bias_own, bias_adj, far)


def _outproj_kernel(x_ref, oa_ref, ob_ref, wa_ref, wb_ref, g_ref, wrh_ref, wrl_ref, br_ref,
                    x1_ref, h2_ref, idx_ref, gate_ref, rank_ref, cnt_ref, run_scr):
    i = pl.program_id(0)

    @pl.when(i == 0)
    def _():
        run_scr[...] = jnp.zeros_like(run_scr)

    x1 = x_ref[...] + _dot(oa_ref[...], wa_ref[...]) + _dot(ob_ref[...], wb_ref[...])
    x1_ref[...] = x1
    ms = jnp.mean(x1 * x1, axis=-1, keepdims=True)
    h2 = x1 * lax.rsqrt(ms + NORM_EPS) * g_ref[...]
    h2_ref[...] = h2
    hh, hl = _split_bf16(h2)
    logits = (_dot(hh, wrh_ref[...]) + _dot(hl, wrh_ref[...]) + _dot(hh, wrl_ref[...])) + br_ref[...]

    tm = logits.shape[0]
    lane = lax.broadcasted_iota(I32, (tm, LANES), 1)
    vals, idxs = [], []
    for _k in range(TOP_K):
        mx = jnp.max(logits, axis=-1, keepdims=True)
        ix = jnp.min(jnp.where(logits == mx, lane, LANES), axis=-1, keepdims=True)
        vals.append(mx)
        idxs.append(ix)
        logits = jnp.where(lane == ix, -3e38, logits)
    es = [jnp.exp(v - vals[0]) for v in vals]
    den = es[0] + es[1] + es[2] + es[3]

    onehots = [lane == ix for ix in idxs]
    ohsum = jnp.zeros((tm, LANES), F32)
    for oh in onehots:
        ohsum = ohsum + jnp.where(oh, 1.0, 0.0)
    r_i = lax.broadcasted_iota(I32, (tm, tm), 0)
    c_i = lax.broadcasted_iota(I32, (tm, tm), 1)
    lower = jnp.where(r_i > c_i, 1.0, 0.0).astype(BF16)
    base = run_scr[0:1, :] + _dot(lower, ohsum.astype(BF16))

    idx_out = jnp.zeros((tm, LANES), I32)
    gate_out = jnp.zeros((tm, LANES), F32)
    rank_out = jnp.zeros((tm, LANES), F32)
    for k in range(TOP_K):
        rk = jnp.sum(jnp.where(onehots[k], base, 0.0), axis=-1, keepdims=True)
        idx_out = jnp.where(lane == k, idxs[k], idx_out)
        gate_out = jnp.where(lane == k, es[k] / den, gate_out)
        rank_out = jnp.where(lane == k, rk, rank_out)
    idx_ref[...] = idx_out
    gate_ref[...] = gate_out
    rank_ref[...] = rank_out.astype(I32)
    run = run_scr[0:1, :] + jnp.sum(ohsum, axis=0, keepdims=True)
    run_scr[...] = jnp.broadcast_to(run, run_scr.shape)
    cnt_ref[...] = jnp.broadcast_to(run, cnt_ref.shape).astype(I32)


def _outproj(x2, oa, ob, wa, wb, g, wr_hi, wr_lo, br, tm=256):
    T, D = x2.shape
    Ka = oa.shape[1]
    Kb = ob.shape[1]
    row = lambda i: (i, 0)
    fix = lambda i: (0, 0)
    return pl.pallas_call(
        _outproj_kernel,
        out_shape=(
            jax.ShapeDtypeStruct((T, D), F32),
            jax.ShapeDtypeStruct((T, D), F32),
            jax.ShapeDtypeStruct((T, LANES), I32),
            jax.ShapeDtypeStruct((T, LANES), F32),
            jax.ShapeDtypeStruct((T, LANES), I32),
            jax.ShapeDtypeStruct((8, LANES), I32),
        ),
        grid=(T // tm,),
        in_specs=[
            pl.BlockSpec((tm, D), row),
            pl.BlockSpec((tm, Ka), row),
            pl.BlockSpec((tm, Kb), row),
            pl.BlockSpec((Ka, D), fix),
            pl.BlockSpec((Kb, D), fix),
            pl.BlockSpec((1, D), fix),
            pl.BlockSpec((D, LANES), fix),
            pl.BlockSpec((D, LANES), fix),
            pl.BlockSpec((1, LANES), fix),
        ],
        out_specs=(
            pl.BlockSpec((tm, D), row),
            pl.BlockSpec((tm, D), row),
            pl.BlockSpec((tm, LANES), row),
            pl.BlockSpec((tm, LANES), row),
            pl.BlockSpec((tm, LANES), row),
            pl.BlockSpec((8, LANES), fix),
        ),
        scratch_shapes=[pltpu.VMEM((8, LANES), F32)],
        compiler_params=_params(("arbitrary",)),
        name="outproj_router",
    )(x2, oa, ob, wa, wb, g, wr_hi, wr_lo, br)


def _gather_kernel(tok_ref, h_hbm, o_ref, buf, sem):
    n = buf.shape[0]

    def issue(r, carry):
        t = tok_ref[0, 0, r]
        pltpu.make_async_copy(h_hbm.at[pl.ds(t, 1)], buf.at[pl.ds(r, 1)], sem).start()
        return carry

    lax.fori_loop(0, n, issue, 0)

    def drain(r, carry):
        pltpu.make_async_copy(h_hbm.at[pl.ds(0, 1)], buf.at[pl.ds(r, 1)], sem).wait()
        return carry

    lax.fori_loop(0, n, drain, 0)
    o_ref[...] = buf[...].astype(BF16)


def _gather_rows(buf_tok, h2, rb=MOE_BLOCK):
    P = buf_tok.shape[0]
    D = h2.shape[1]
    return pl.pallas_call(
        _gather_kernel,
        out_shape=jax.ShapeDtypeStruct((P, D), BF16),
        grid=(P // rb,),
        in_specs=[
            pl.BlockSpec((1, 1, rb), lambda i: (i, 0, 0), memory_space=pltpu.SMEM),
            pl.BlockSpec(memory_space=pl.ANY),
        ],
        out_specs=pl.BlockSpec((rb, D), lambda i: (i, 0)),
        scratch_shapes=[pltpu.VMEM((rb, D), F32), pltpu.SemaphoreType.DMA(())],
        compiler_params=_params(("arbitrary",)),
        name="moe_gather",
    )(buf_tok.reshape(P // rb, 1, rb), h2)


_GU_TN = 2048
_PERM_W = 256


def _moe_gu_kernel(be_ref, xs_ref, w_ref, bg_ref, bl_ref, perm_ref, o_ref, w_scr):
    i = pl.program_id(1)
    prev = be_ref[jnp.maximum(i - 1, 0)]
    changed = jnp.logical_or(i == 0, be_ref[i] != prev)
    K = w_ref.shape[1]
    kr = 512

    @pl.when(changed)
    def _():
        pm = perm_ref[...]
        for r0 in range(0, K, kr):
            for g in range(_GU_TN // _PERM_W):
                wt = w_ref[0, r0:r0 + kr, g * _PERM_W:(g + 1) * _PERM_W].astype(BF16)
                w_scr[r0:r0 + kr, g * _PERM_W:(g + 1) * _PERM_W] = _dot(wt, pm).astype(BF16)

    gu = _dot(xs_ref[...], w_scr[...])
    hw = _PERM_W // 2
    for g in range(_GU_TN // _PERM_W):
        sl = slice(g * hw, (g + 1) * hw)
        xg = gu[:, g * _PERM_W:g * _PERM_W + hw] + bg_ref[0, :, sl]
        xl = gu[:, g * _PERM_W + hw:(g + 1) * _PERM_W] + bl_ref[0, :, sl]
        xg = jnp.minimum(xg, SWIGLU_LIMIT)
        xl = jnp.clip(xl, -SWIGLU_LIMIT, SWIGLU_LIMIT)
        act = xg * jax.nn.sigmoid(SWIGLU_ALPHA * xg) * (xl + 1.0)
        o_ref[:, sl] = act.astype(BF16)


def _moe_gu(block_expert, xs, w_gu, b_gate, b_lin, bm=MOE_BLOCK):
    P, D = xs.shape
    E, _, N2 = w_gu.shape
    F = N2 // 2
    nj = N2 // _GU_TN
    fo = _GU_TN // 2
    idx = np.arange(_PERM_W)
    src = np.where(idx < _PERM_W // 2, 2 * idx, 2 * (idx - _PERM_W // 2) + 1)
    pm = np.zeros((_PERM_W, _PERM_W), np.float32)
    pm[src, idx] = 1.0
    perm = jnp.asarray(pm, dtype=BF16)
    return pl.pallas_call(
        _moe_gu_kernel,
        out_shape=jax.ShapeDtypeStruct((P, F), BF16),
        grid_spec=pltpu.PrefetchScalarGridSpec(
            num_scalar_prefetch=1,
            grid=(nj, P // bm),
            in_specs=[
                pl.BlockSpec((bm, D), lambda j, i, be: (i, 0)),
                pl.BlockSpec((1, D, _GU_TN), lambda j, i, be: (be[i], 0, j)),
                pl.BlockSpec((1, 1, fo), lambda j, i, be: (be[i], 0, j)),
                pl.BlockSpec((1, 1, fo), lambda j, i, be: (be[i], 0, j)),
                pl.BlockSpec((_PERM_W, _PERM_W), lambda j, i, be: (0, 0)),
            ],
            out_specs=pl.BlockSpec((bm, fo), lambda j, i, be: (i, j)),
            scratch_shapes=[pltpu.VMEM((D, _GU_TN), BF16)],
        ),
        compiler_params=_params(("arbitrary", "arbitrary")),
        name="moe_gate_up",
    )(block_expert, xs, w_gu, b_gate, b_lin, perm)


def _moe_down_kernel(be_ref, h_ref, w_ref, b_ref, o_ref, w_scr):
    i = pl.program_id(0)
    prev = be_ref[jnp.maximum(i - 1, 0)]
    changed = jnp.logical_or(i == 0, be_ref[i] != prev)

    @pl.when(changed)
    def _():
        w_scr[...] = w_ref[0].astype(BF16)

    o_ref[...] = _dot(h_ref[...], w_scr[...]) + b_ref[0]


def _moe_down(block_expert, hact, w_down, b_down, bm=MOE_BLOCK):
    P, F = hact.shape
    E, _, D = w_down.shape
    return pl.pallas_call(
        _moe_down_kernel,
        out_shape=jax.ShapeDtypeStruct((P, D), F32),
        grid_spec=pltpu.PrefetchScalarGridSpec(
            num_scalar_prefetch=1,
            grid=(P // bm,),
            in_specs=[
                pl.BlockSpec((bm, F), lambda i, be: (i, 0)),
                pl.BlockSpec((1, F, D), lambda i, be: (be[i], 0, 0)),
                pl.BlockSpec((1, 1, D), lambda i, be: (be[i], 0, 0)),
            ],
            out_specs=pl.BlockSpec((bm, D), lambda i, be: (i, 0)),
            scratch_shapes=[pltpu.VMEM((F, D), BF16)],
        ),
        compiler_params=_params(("arbitrary",)),
        name="moe_down",
    )(block_expert, hact, w_down, b_down)


def _combine_kernel(dest_ref, x1_ref, gate_ref, y_hbm, o_ref, buf, sem):
    tb = x1_ref.shape[0]

    def issue(a, carry):
        d = dest_ref[0, 0, a]
        pltpu.make_async_copy(y_hbm.at[pl.ds(d, 1)], buf.at[pl.ds(a, 1)], sem).start()
        return carry

    lax.fori_loop(0, TOP_K * tb, issue, 0)

    def drain(a, carry):
        pltpu.make_async_copy(y_hbm.at[pl.ds(0, 1)], buf.at[pl.ds(a, 1)], sem).wait()
        return carry

    lax.fori_loop(0, TOP_K * tb, drain, 0)
    acc = x1_ref[...]
    gates = gate_ref[...]
    for k in range(TOP_K):
        acc = acc + buf[k * tb:(k + 1) * tb, :] * gates[:, k:k + 1]
    o_ref[...] = acc


def _combine(dest_km, x1, gates, y, tb=64):
    T, D = x1.shape
    return pl.pallas_call(
        _combine_kernel,
        out_shape=jax.ShapeDtypeStruct((T, D), F32),
        grid=(T // tb,),
        in_specs=[
            pl.BlockSpec((1, 1, TOP_K * tb), lambda i: (i, 0, 0), memory_space=pltpu.SMEM),
            pl.BlockSpec((tb, D), lambda i: (i, 0)),
            pl.BlockSpec((tb, LANES), lambda i: (i, 0)),
            pl.BlockSpec(memory_space=pl.ANY),
        ],
        out_specs=pl.BlockSpec((tb, D), lambda i: (i, 0)),
        scratch_shapes=[pltpu.VMEM((TOP_K * tb, D), F32), pltpu.SemaphoreType.DMA(())],
        compiler_params=_params(("arbitrary",)),
        name="moe_combine",
    )(dest_km, x1, gates, y)


def _t5_bucket(dist):
    max_exact = T5_BUCKETS // 2
    d = jnp.maximum(dist, 0)
    df = jnp.maximum(d, 1).astype(F32)
    large = max_exact + (jnp.log(df / max_exact) / math.log(T5_MAX_DISTANCE / max_exact)
                         * (T5_BUCKETS - max_exact)).astype(I32)
    large = jnp.minimum(large, T5_BUCKETS - 1)
    return jnp.where(d < max_exact, d, large)


def _toeplitz_bias(rel_t, dist, valid):
    b = rel_t[:, _t5_bucket(jnp.asarray(dist, I32))]
    return jnp.where(jnp.asarray(valid)[None], b, NEG).astype(F32)


def _pair_major(a, n_pair, n_half, n_c):
    sh = a.shape
    a = a.reshape((n_pair, n_half, n_c) + sh[1:])
    a = jnp.swapaxes(a, 1, 2)
    return a.reshape(sh)


def kernel(x, attn_norm_g, w_in, swa_q_gain, swa_k_gain, swa_sinks, moba_q_gain, moba_k_gain, rel_bias,
           w_out, ffn_norm_g, w_router, b_router, w_gate_up, b_gate_up, w_down, b_down):
    B, S, D = x.shape
    T = B * S
    assert w_in.shape[0] == 1, "single-layer kernel"
    qa_w = SWA_Q_HEADS * HEAD_DIM
    kv_w = SWA_KV_HEADS * HEAD_DIM
    mb_w = MOBA_HEADS * HEAD_DIM
    G = SWA_Q_HEADS // SWA_KV_HEADS
    npair = SWA_KV_HEADS // 2

    w0 = w_in[0]
    wq = w0[:, :qa_w].reshape(D, SWA_Q_HEADS, HEAD_DIM)
    wq = jnp.swapaxes(wq.reshape(D, npair, 2, G, HEAD_DIM), 2, 3).reshape(D, qa_w)
    w_in_bf = jnp.concatenate([wq, w0[:, qa_w:]], axis=1).astype(BF16)
    ones = jnp.ones((HEAD_DIM,), F32)
    tile = lambda v, n: jnp.tile(v.astype(F32), n)
    gain_col = jnp.concatenate([
        tile(swa_q_gain[0] * ATTN_SCALE, SWA_Q_HEADS), tile(swa_k_gain[0], SWA_KV_HEADS),
        tile(ones, SWA_KV_HEADS), tile(moba_q_gain[0] * ATTN_SCALE, MOBA_HEADS),
        tile(moba_k_gain[0], MOBA_HEADS), tile(ones, MOBA_HEADS)])[None, :]
    flag_np = np.concatenate([np.ones(qa_w + kv_w), np.zeros(kv_w), np.ones(2 * mb_w), np.zeros(mb_w)])
    flag_col = jnp.asarray(flag_np[None, :], F32)

    wo = w_out[0]
    wa = wo[:qa_w].reshape(npair, 2, G, HEAD_DIM, D)
    wa = jnp.swapaxes(wa, 1, 2).reshape(qa_w, D).astype(BF16)
    wb = wo[qa_w:].astype(BF16)

    rel_a = rel_bias[:, :SWA_Q_HEADS].T.astype(F32)
    rel_b = rel_bias[:, SWA_Q_HEADS:].T.astype(F32)
    W = SWA_WINDOW
    r = np.arange(W)[:, None]
    cc = np.arange(2 * W)[None, :]
    dist = r + W - cc
    bias_a = _toeplitz_bias(rel_a, dist, (dist >= 0) & (dist < W))
    bias_swa = _pair_major(bias_a, npair, 2, G).reshape(npair, G, 2 * W, 2 * W)
    sink_col = jnp.broadcast_to(swa_sinks[0].astype(F32)[:, None, None], (SWA_Q_HEADS, W, 1))
    sink_swa = _pair_major(sink_col, npair, 2, G).reshape(npair, G, 2 * W, 1)
    L = MOBA_BLOCK
    rr = np.arange(L)[:, None]
    kk = np.arange(L)[None, :]
    bias_own = _toeplitz_bias(rel_b, rr - kk, rr >= kk)
    bias_adj = _toeplitz_bias(rel_b, rr - kk + L, np.ones((L, L), bool))
    far = jnp.broadcast_to(rel_b[:, T5_BUCKETS - 1][:, None, None], (MOBA_HEADS, 1, LANES))

    x2 = x.reshape(T, D)
    proj = _inproj(x2, attn_norm_g[0][None, :].astype(F32), w_in_bf, gain_col, flag_col)
    proj3 = proj.reshape(B, S, proj.shape[1])
    kmean = _kmean(proj3)
    nblk = S // L
    assert nblk <= _GATE_SLOTS, "selection mask packs at most 16 key blocks per head"
    km = kmean.reshape(B, nblk, MOBA_HEADS, HEAD_DIM).transpose(0, 2, 3, 1)
    km = jnp.pad(km, ((0, 0), (0, 0), (0, 0), (0, _GATE_SLOTS - nblk)))
    eye = jnp.eye(MOBA_HEADS, dtype=F32)
    kmbd = (km[:, :, :, None, :] * eye[None, :, None, :, None]).reshape(B, mb_w, MOBA_HEADS * _GATE_SLOTS)
    kmbd_hi, kmbd_lo = _split_bf16(kmbd)
    mask = _select(proj3, kmbd_hi, kmbd_lo)
    oa = _swa(proj3, bias_swa, sink_swa)
    ob = _moba(proj3, mask, bias_own, bias_adj, far)

    wr = jnp.pad(w_router[0].astype(F32), ((0, 0), (0, LANES - N_EXPERTS)))
    wr_hi, wr_lo = _split_bf16(wr)
    br = jnp.pad(b_router[0].astype(F32), (0, LANES - N_EXPERTS), constant_values=NEG)[None, :]
    x1, h2, idx_o, gate_o, rank_o, cnt_o = _outproj(
        x2, oa.reshape(T, qa_w), ob.reshape(T, mb_w), wa, wb, ffn_norm_g[0][None, :].astype(F32),
        wr_hi, wr_lo, br)

    A = T * TOP_K
    counts = cnt_o[0, :N_EXPERTS]
    padded = (counts + MOE_BLOCK - 1) // MOE_BLOCK * MOE_BLOCK
    pad_ends = jnp.cumsum(padded)
    pad_starts = pad_ends - padded
    n_blocks = -(-(A + N_EXPERTS * (MOE_BLOCK - 1)) // MOE_BLOCK)
    P = n_blocks * MOE_BLOCK
    top_idx = idx_o[:, :TOP_K]
    dest = pad_starts[top_idx] + rank_o[:, :TOP_K]
    tok = jnp.broadcast_to(jnp.arange(T, dtype=I32)[:, None], (T, TOP_K))
    buf_tok = jnp.zeros((P,), I32).at[dest.reshape(A)].set(tok.reshape(A))
    block_expert = jnp.minimum(
        jnp.searchsorted(pad_ends, jnp.arange(n_blocks, dtype=I32) * MOE_BLOCK, side='right'),
        N_EXPERTS - 1).astype(I32)

    xs = _gather_rows(buf_tok, h2)
    bgu = b_gate_up[0].astype(F32)
    b_gate = bgu[:, 0::2][:, None, :]
    b_lin = bgu[:, 1::2][:, None, :]
    hact = _moe_gu(block_expert, xs, w_gate_up[0], b_gate, b_lin)
    y = _moe_down(block_expert, hact, w_down[0], b_down[0].astype(F32)[:, None, :])
    tb = 64
    dest_km = dest.reshape(T // tb, tb, TOP_K).transpose(0, 2, 1).reshape(T // tb, 1, TOP_K * tb)
    out = _combine(dest_km, x1, gate_o, y, tb=tb)
    return out.reshape(B, S, D)
```

```python
import functools
import math

import numpy as np
import jax
import jax.numpy as jnp
from jax import lax
from jax.experimental import pallas as pl
from jax.experimental.pallas import tpu as pltpu

F32 = jnp.float32
BF16 = jnp.bfloat16
I32 = jnp.int32

HEAD_DIM = 64
SWA_Q_HEADS = 16
SWA_KV_HEADS = 4
SWA_WINDOW = 128
MOBA_HEADS = 16
MOBA_BLOCK = 256
MOBA_TOPK = 3
T5_BUCKETS = 32
T5_MAX_DISTANCE = 128
N_EXPERTS = 32
TOP_K = 4
SWIGLU_LIMIT = 7.0
SWIGLU_ALPHA = 1.702
MOE_BLOCK = 256
NORM_EPS = 1e-5
ATTN_SCALE = HEAD_DIM ** -0.5

LANES = 128
NEG = -1e30
VMEM_LIMIT = 56 * 1024 * 1024
_GATE_SLOTS = 16
_HEADS_PER_MASK_BLOCK = LANES // _GATE_SLOTS

_QA, _KA, _VA, _QB, _KB, _VB = 0, 8, 10, 12, 20, 28


def _dot(a, b):
    return jnp.dot(a, b, preferred_element_type=F32)


def _dot_nt(a, b):
    return lax.dot_general(a, b, (((1,), (1,)), ((), ())), preferred_element_type=F32)


def _split_bf16(x):
    hi = x.astype(BF16)
    lo = (x - hi.astype(F32)).astype(BF16)
    return hi, lo


def _head_lane_masks(rows):
    lane = lax.broadcasted_iota(I32, (rows, LANES), 1)
    lo = jnp.where(lane < HEAD_DIM, 1.0, 0.0).astype(BF16)
    return lo, (1.0 - lo.astype(F32)).astype(BF16)


def _store_slabs(ref, val):
    for c in range(ref.shape[1]):
        ref[:, c, :] = val[:, c * LANES:(c + 1) * LANES]


def _load_slabs(ref, rows=None):
    rs = slice(None) if rows is None else rows
    return jnp.concatenate([ref[rs, c, :] for c in range(ref.shape[1])], axis=1)


def _params(sem):
    return pltpu.CompilerParams(dimension_semantics=sem, vmem_limit_bytes=VMEM_LIMIT)


def _inproj_kernel(x_ref, g_ref, w_ref, gain_ref, flag_ref, bd_ref, o_ref, h_scr):
    @pl.when(pl.program_id(1) == 0)
    def _():
        x = x_ref[...]
        ms = jnp.mean(x * x, axis=-1, keepdims=True)
        h_scr[...] = (x * lax.rsqrt(ms + NORM_EPS) * g_ref[...]).astype(BF16)

    y = _dot(h_scr[...], w_ref[...])
    bd = bd_ref[...]
    for c in range(y.shape[1] // LANES):
        yc = y[:, c * LANES:(c + 1) * LANES]
        hi, lo = _split_bf16(yc * yc)
        ssum = _dot(hi, bd) + _dot(lo, bd)
        r = lax.rsqrt(ssum * (1.0 / HEAD_DIM) + NORM_EPS)
        sl = slice(c * LANES, (c + 1) * LANES)
        scale = jnp.where(flag_ref[:, sl] > 0.5, r, 1.0) * gain_ref[:, sl]
        o_ref[:, sl] = (yc * scale).astype(BF16)


def _inproj(x2, g, w_bf, gain_col, flag_col, tm=512, tn=1536):
    T, D = x2.shape
    N = w_bf.shape[1]
    blk = np.kron(np.eye(LANES // HEAD_DIM), np.ones((HEAD_DIM, HEAD_DIM))).astype(np.float32)
    bd = jnp.asarray(blk, dtype=BF16)
    return pl.pallas_call(
        _inproj_kernel,
        out_shape=jax.ShapeDtypeStruct((T, N), BF16),
        grid=(T // tm, N // tn),
        in_specs=[
            pl.BlockSpec((tm, D), lambda i, j: (i, 0)),
            pl.BlockSpec((1, D), lambda i, j: (0, 0)),
            pl.BlockSpec((D, tn), lambda i, j: (0, j)),
            pl.BlockSpec((1, tn), lambda i, j: (0, j)),
            pl.BlockSpec((1, tn), lambda i, j: (0, j)),
            pl.BlockSpec((LANES, LANES), lambda i, j: (0, 0)),
        ],
        out_specs=pl.BlockSpec((tm, tn), lambda i, j: (i, j)),
        scratch_shapes=[pltpu.VMEM((tm, D), BF16)],
        compiler_params=_params(("arbitrary", "arbitrary")),
        name="inproj",
    )(x2, g, w_bf, gain_col, flag_col, bd)


def _kmean_kernel(k_ref, o_ref):
    nblk = o_ref.shape[1]
    for j in range(nblk):
        kj = k_ref[0, j * MOBA_BLOCK:(j + 1) * MOBA_BLOCK, :].astype(F32)
        o_ref[0, j:j + 1, :] = jnp.sum(kj, axis=0, keepdims=True) * (1.0 / MOBA_BLOCK)


def _kmean(proj3):
    B, S, _ = proj3.shape
    nblk = S // MOBA_BLOCK
    W = MOBA_HEADS * HEAD_DIM
    return pl.pallas_call(
        _kmean_kernel,
        out_shape=jax.ShapeDtypeStruct((B, nblk, W), F32),
        grid=(B, W // 512),
        in_specs=[pl.BlockSpec((1, S, 512), lambda b, w: (b, 0, _KB * LANES // 512 + w))],
        out_specs=pl.BlockSpec((1, nblk, 512), lambda b, w: (b, 0, w)),
        compiler_params=_params(("arbitrary", "arbitrary")),
        name="kmean",
    )(proj3)


def _select_kernel(qlo_ref, qhi_ref, kmh_ref, kml_ref, o_ref):
    c = pl.program_id(1)
    half_k = qlo_ref.shape[2]
    qlo = qlo_ref[0]
    qhi = qhi_ref[0]
    g = (_dot(qlo, kmh_ref[0, :half_k, :]) + _dot(qhi, kmh_ref[0, half_k:, :])
         + _dot(qlo, kml_ref[0, :half_k, :]) + _dot(qhi, kml_ref[0, half_k:, :]))
    nb = _GATE_SLOTS
    lane = lax.broadcasted_iota(I32, (g.shape[0], LANES), 1)
    j = lane & (nb - 1)
    for half in range(g.shape[1] // LANES):
        gh = g[:, half * LANES:(half + 1) * LANES]
        cnt = jnp.zeros(gh.shape, F32)
        for d in range(1, nb):
            a = pltpu.roll(gh, LANES - d, 1)
            b = pltpu.roll(gh, nb - d, 1)
            wrap = (j + d) >= nb
            partner = jnp.where(wrap, b, a)
            jp = jnp.where(wrap, j + (d - nb), j + d)
            ahead = jnp.where(jp < j, jnp.where(partner >= gh, 1.0, 0.0), jnp.where(partner > gh, 1.0, 0.0))
            cnt = cnt + jnp.where(jp < c, ahead, 0.0)
        sel = jnp.where(j < c, cnt, 1e9) < (MOBA_TOPK - 0.5)
        o_ref[0, :, half * LANES:(half + 1) * LANES] = jnp.where(sel, 0.0, NEG).astype(BF16)


def _select(proj3, kmbd_hi, kmbd_lo):
    B, S, _ = proj3.shape
    nq = S // MOBA_BLOCK
    KW = MOBA_HEADS * HEAD_DIM
    NW = kmbd_hi.shape[2]
    qb0 = _QB * LANES // 512
    return pl.pallas_call(
        _select_kernel,
        out_shape=jax.ShapeDtypeStruct((B, S, NW), BF16),
        grid=(B, nq),
        in_specs=[
            pl.BlockSpec((1, MOBA_BLOCK, 512), lambda b, c: (b, c, qb0)),
            pl.BlockSpec((1, MOBA_BLOCK, 512), lambda b, c: (b, c, qb0 + 1)),
            pl.BlockSpec((1, KW, NW), lambda b, c: (b, 0, 0)),
            pl.BlockSpec((1, KW, NW), lambda b, c: (b, 0, 0)),
        ],
        out_specs=pl.BlockSpec((1, MOBA_BLOCK, NW), lambda b, c: (b, c, 0)),
        compiler_params=_params(("arbitrary", "arbitrary")),
        name="moba_select",
    )(proj3, proj3, kmbd_hi, kmbd_lo)


def _swa_kernel(q_ref, kc_ref, kp_ref, vc_ref, vp_ref, bias_ref, sink_ref, o_ref, k_scr, v_scr):
    W = SWA_WINDOW
    tq = q_ref.shape[1]
    first = pl.program_id(2) == 0
    k_scr[0:W, :] = kp_ref[0]
    k_scr[W:, :] = kc_ref[0]
    v_scr[0:W, :] = vp_ref[0]
    v_scr[W:, :] = vc_ref[0]
    lane = lax.broadcasted_iota(I32, (W, LANES), 1)
    keep_lo, keep_hi = _head_lane_masks(W)
    col = lax.broadcasted_iota(I32, (1, 2 * W), 1)
    nopast = jnp.where(jnp.logical_and(first, col < W), NEG, 0.0)
    for sb in range(tq // W):
        k2 = k_scr[sb * W:sb * W + 2 * W, :]
        v2 = v_scr[sb * W:sb * W + 2 * W, :]
        for c in range(q_ref.shape[2] // LANES):
            qc = q_ref[0, sb * W:(sb + 1) * W, c * LANES:(c + 1) * LANES]
            qab = jnp.concatenate([qc * keep_lo, qc * keep_hi], axis=0)
            s = _dot_nt(qab, k2) + bias_ref[0, c]
            if sb == 0:
                s = s + nopast
            sink = sink_ref[0, c]
            m = jnp.maximum(jnp.max(s, axis=-1, keepdims=True), sink)
            p = jnp.exp(s - m)
            l = jnp.sum(p, axis=-1, keepdims=True) + jnp.exp(sink - m)
            o = _dot(p.astype(BF16), v2) / l
            o_ref[0, sb * W:(sb + 1) * W, c * LANES:(c + 1) * LANES] = jnp.where(
                lane < HEAD_DIM, o[:W], o[W:]).astype(BF16)


def _swa(proj3, bias_swa, sink_swa, tq=512):
    B, S, _ = proj3.shape
    W = SWA_WINDOW
    r = tq // W
    npair = SWA_KV_HEADS // 2
    qw = SWA_Q_HEADS * HEAD_DIM // npair
    nc = qw // LANES
    return pl.pallas_call(
        _swa_kernel,
        out_shape=jax.ShapeDtypeStruct((B, S, SWA_Q_HEADS * HEAD_DIM), BF16),
        grid=(B, npair, S // tq),
        in_specs=[
            pl.BlockSpec((1, tq, qw), lambda b, p, i: (b, i, p)),
            pl.BlockSpec((1, tq, LANES), lambda b, p, i: (b, i, _KA + p)),
            pl.BlockSpec((1, W, LANES), lambda b, p, i: (b, jnp.maximum(i * r - 1, 0), _KA + p)),
            pl.BlockSpec((1, tq, LANES), lambda b, p, i: (b, i, _VA + p)),
            pl.BlockSpec((1, W, LANES), lambda b, p, i: (b, jnp.maximum(i * r - 1, 0), _VA + p)),
            pl.BlockSpec((1, nc, 2 * W, 2 * W), lambda b, p, i: (p, 0, 0, 0)),
            pl.BlockSpec((1, nc, 2 * W, 1), lambda b, p, i: (p, 0, 0, 0)),
        ],
        out_specs=pl.BlockSpec((1, tq, qw), lambda b, p, i: (b, i, p)),
        scratch_shapes=[pltpu.VMEM((tq + W, LANES), BF16), pltpu.VMEM((tq + W, LANES), BF16)],
        compiler_params=_params(("arbitrary", "arbitrary", "arbitrary")),
        name="swa_attn",
    )(proj3, proj3, proj3, proj3, proj3, bias_swa, sink_swa)


_MOBA_HG = _HEADS_PER_MASK_BLOCK


_VROWS = 80


def _moba_kernel(q_ref, k_ref, vt_ref, m_ref, bo_ref, ba_ref, far_ref, o_ref,
                 qaug_scr, sa_scr, sb_scr, m_scr, acc_scr):
    L = MOBA_BLOCK
    c = pl.program_id(2)
    lane = lax.broadcasted_iota(I32, (L, LANES), 1)
    keep = _head_lane_masks(L)
    mblk = m_ref[0]

    def rows(j):
        return pl.ds(pl.multiple_of(j * L, L), L)

    def pair(hl):
        return slice((hl // 2) * LANES, (hl // 2 + 1) * LANES)

    for hl in range(_MOBA_HG):
        qaug_scr[hl, :, 0:LANES] = q_ref[0, :, pair(hl)] * keep[hl % 2]
        qaug_scr[hl, :, LANES:2 * LANES] = mblk
    m_scr[...] = jnp.full(m_scr.shape, NEG, F32)
    acc_scr[...] = jnp.zeros(acc_scr.shape, F32)

    def scores(hl, j, s_buf, masked=True):
        kj = k_ref[0, rows(j), pair(hl)]
        if masked:
            onehot = jnp.where(lane == hl * _GATE_SLOTS + j, 1.0, 0.0).astype(BF16)
            s_buf[hl] = _dot_nt(jnp.concatenate([kj, onehot], axis=1), qaug_scr[hl])
        else:
            s_buf[hl] = _dot_nt(kj, qaug_scr[hl, :, 0:LANES])

    def softmax_pv(hl, j, s_buf, kind):
        s = s_buf[hl]
        cb = 0.0
        if kind == "own":
            s = s + bo_ref[hl]
        elif kind == "adj":
            s = s + ba_ref[hl]
        else:
            cb = far_ref[hl, :, 0:1]
        m_old = m_scr[hl]
        m_new = jnp.maximum(m_old, jnp.max(s, axis=0, keepdims=True) + cb)
        pr = jnp.exp(s - (m_new - cb)).astype(BF16)
        m_scr[hl] = m_new
        vt = vt_ref[0, hl * _VROWS:(hl + 1) * _VROWS, rows(j)]
        acc_scr[hl] = acc_scr[hl] * jnp.exp(m_old - m_new) + _dot(vt, pr)

    def stage(qk, sm):
        for hl in range(_MOBA_HG):
            if qk is not None:
                scores(hl, *qk)
            if sm is not None:
                softmax_pv(hl, *sm)

    n_far = jnp.maximum(c - 1, 0)
    n_pair = (n_far + 1) // 2

    @pl.when(c >= 1)
    def _():
        stage((0, sa_scr), None)

    def far_body(tt, carry):
        t0 = 2 * tt
        j1 = jnp.where(t0 + 1 < n_far, t0 + 1, c)
        stage((j1, sb_scr), (t0, sa_scr, "far"))
        stage((jnp.minimum(t0 + 2, c - 1), sa_scr), (j1, sb_scr, "far"))
        return carry

    lax.fori_loop(0, n_pair, far_body, 0)

    @pl.when(c >= 1)
    def _():
        stage((c, sb_scr, False), (c - 1, sa_scr, "adj"))

    @pl.when(c == 0)
    def _():
        stage((c, sb_scr, False), None)

    stage(None, (c, sb_scr, "own"))

    for pr_ in range(_MOBA_HG // 2):
        a0 = acc_scr[2 * pr_]
        a1 = acc_scr[2 * pr_ + 1]
        ot = jnp.concatenate([a0[:HEAD_DIM] / a0[HEAD_DIM:HEAD_DIM + 1],
                              a1[:HEAD_DIM] / a1[HEAD_DIM:HEAD_DIM + 1]], axis=0)
        o_ref[0, :, pr_ * LANES:(pr_ + 1) * LANES] = ot.T.astype(BF16)


def _moba(proj3, vbt, mask, bias_own_t, bias_adj_t, far):
    B, S, _ = proj3.shape
    L = MOBA_BLOCK
    hg = _MOBA_HG
    gw = hg * HEAD_DIM
    ng = MOBA_HEADS // hg
    return pl.pallas_call(
        _moba_kernel,
        out_shape=jax.ShapeDtypeStruct((B, S, MOBA_HEADS * HEAD_DIM), BF16),
        grid=(B, ng, S // L),
        in_specs=[
            pl.BlockSpec((1, L, gw), lambda b, g, c: (b, c, _QB * LANES // gw + g)),
            pl.BlockSpec((1, S, gw), lambda b, g, c: (b, 0, _KB * LANES // gw + g)),
            pl.BlockSpec((1, hg * _VROWS, S), lambda b, g, c: (b, g, 0)),
            pl.BlockSpec((1, L, LANES), lambda b, g, c: (b, c, g)),
            pl.BlockSpec((hg, L, L), lambda b, g, c: (g, 0, 0)),
            pl.BlockSpec((hg, L, L), lambda b, g, c: (g, 0, 0)),
            pl.BlockSpec((hg, 1, LANES), lambda b, g, c: (g, 0, 0)),
        ],
        out_specs=pl.BlockSpec((1, L, gw), lambda b, g, c: (b, c, g)),
        scratch_shapes=[pltpu.VMEM((hg, L, 2 * LANES), BF16), pltpu.VMEM((hg, L, L), F32),
                        pltpu.VMEM((hg, L, L), F32), pltpu.VMEM((hg, 1, L), F32),
                        pltpu.VMEM((hg, _VROWS, L), F32)],
        compiler_params=_params(("arbitrary", "arbitrary", "arbitrary")),
        name="moba_attn",
    )(proj3, proj3, vbt, mask, bias_own_t, bias_adj_t, far)


def _outproj_kernel(x_ref, oa_ref, ob_ref, wa_ref, wb_ref, g_ref, wrh_ref, wrl_ref, br_ref,
                    x1_ref, h2_ref, idx_ref, gate_ref, rank_ref, cnt_ref, run_scr):
    i = pl.program_id(0)

    @pl.when(i == 0)
    def _():
        run_scr[...] = jnp.zeros_like(run_scr)

    x1 = x_ref[...] + _dot(oa_ref[...], wa_ref[...]) + _dot(ob_ref[...], wb_ref[...])
    x1_ref[...] = x1
    ms = jnp.mean(x1 * x1, axis=-1, keepdims=True)
    h2 = x1 * lax.rsqrt(ms + NORM_EPS) * g_ref[...]
    _store_slabs(h2_ref, h2)
    hh, hl = _split_bf16(h2)
    logits = (_dot(hh, wrh_ref[...]) + _dot(hl, wrh_ref[...]) + _dot(hh, wrl_ref[...])) + br_ref[...]

    tm = logits.shape[0]
    lane = lax.broadcasted_iota(I32, (tm, LANES), 1)
    vals, idxs = [], []
    for _k in range(TOP_K):
        mx = jnp.max(logits, axis=-1, keepdims=True)
        ix = jnp.min(jnp.where(logits == mx, lane, LANES), axis=-1, keepdims=True)
        vals.append(mx)
        idxs.append(ix)
        logits = jnp.where(lane == ix, -3e38, logits)
    es = [jnp.exp(v - vals[0]) for v in vals]
    den = es[0] + es[1] + es[2] + es[3]

    onehots = [lane == ix for ix in idxs]
    ohsum = jnp.zeros((tm, LANES), F32)
    for oh in onehots:
        ohsum = ohsum + jnp.where(oh, 1.0, 0.0)
    r_i = lax.broadcasted_iota(I32, (tm, tm), 0)
    c_i = lax.broadcasted_iota(I32, (tm, tm), 1)
    lower = jnp.where(r_i > c_i, 1.0, 0.0).astype(BF16)
    base = run_scr[0:1, :] + _dot(lower, ohsum.astype(BF16))

    idx_out = jnp.zeros((tm, LANES), I32)
    gate_out = jnp.zeros((tm, LANES), F32)
    rank_out = jnp.zeros((tm, LANES), F32)
    for k in range(TOP_K):
        rk = jnp.sum(jnp.where(onehots[k], base, 0.0), axis=-1, keepdims=True)
        idx_out = jnp.where(lane == k, idxs[k], idx_out)
        gate_out = jnp.where(lane == k, es[k] / den, gate_out)
        rank_out = jnp.where(lane == k, rk, rank_out)
    idx_ref[...] = idx_out
    gate_ref[...] = gate_out
    rank_ref[...] = rank_out.astype(I32)
    run = run_scr[0:1, :] + jnp.sum(ohsum, axis=0, keepdims=True)
    run_scr[...] = jnp.broadcast_to(run, run_scr.shape)
    cnt_ref[...] = jnp.broadcast_to(run, cnt_ref.shape).astype(I32)


def _outproj(x2, oa, ob, wa, wb, g, wr_hi, wr_lo, br, tm=256):
    T, D = x2.shape
    Ka = oa.shape[1]
    Kb = ob.shape[1]
    row = lambda i: (i, 0)
    fix = lambda i: (0, 0)
    return pl.pallas_call(
        _outproj_kernel,
        out_shape=(
            jax.ShapeDtypeStruct((T, D), F32),
            jax.ShapeDtypeStruct((T, D // LANES, LANES), F32),
            jax.ShapeDtypeStruct((T, LANES), I32),
            jax.ShapeDtypeStruct((T, LANES), F32),
            jax.ShapeDtypeStruct((T, LANES), I32),
            jax.ShapeDtypeStruct((8, LANES), I32),
        ),
        grid=(T // tm,),
        in_specs=[
            pl.BlockSpec((tm, D), row),
            pl.BlockSpec((tm, Ka), row),
            pl.BlockSpec((tm, Kb), row),
            pl.BlockSpec((Ka, D), fix),
            pl.BlockSpec((Kb, D), fix),
            pl.BlockSpec((1, D), fix),
            pl.BlockSpec((D, LANES), fix),
            pl.BlockSpec((D, LANES), fix),
            pl.BlockSpec((1, LANES), fix),
        ],
        out_specs=(
            pl.BlockSpec((tm, D), row),
            pl.BlockSpec((tm, D // LANES, LANES), lambda i: (i, 0, 0)),
            pl.BlockSpec((tm, LANES), row),
            pl.BlockSpec((tm, LANES), row),
            pl.BlockSpec((tm, LANES), row),
            pl.BlockSpec((8, LANES), fix),
        ),
        scratch_shapes=[pltpu.VMEM((8, LANES), F32)],
        compiler_params=_params(("arbitrary",)),
        name="outproj_router",
    )(x2, oa, ob, wa, wb, g, wr_hi, wr_lo, br)


def _gather_kernel(tok_ref, h_hbm, o_ref, buf, sem):
    n = buf.shape[0]

    def issue(r, carry):
        t = tok_ref[0, 0, r]
        pltpu.make_async_copy(h_hbm.at[t], buf.at[r], sem).start()
        return carry

    lax.fori_loop(0, n, issue, 0)

    def drain(r, carry):
        pltpu.make_async_copy(h_hbm.at[0], buf.at[r], sem).wait()
        return carry

    lax.fori_loop(0, n, drain, 0)
    o_ref[...] = _load_slabs(buf).astype(BF16)


def _gather_rows(buf_tok, h2, rb=MOE_BLOCK):
    P = buf_tok.shape[0]
    D = h2.shape[1] * h2.shape[2]
    return pl.pallas_call(
        _gather_kernel,
        out_shape=jax.ShapeDtypeStruct((P, D), BF16),
        grid=(P // rb,),
        in_specs=[
            pl.BlockSpec((1, 1, rb), lambda i: (i, 0, 0), memory_space=pltpu.SMEM),
            pl.BlockSpec(memory_space=pl.ANY),
        ],
        out_specs=pl.BlockSpec((rb, D), lambda i: (i, 0)),
        scratch_shapes=[pltpu.VMEM((rb,) + h2.shape[1:], F32), pltpu.SemaphoreType.DMA(())],
        compiler_params=_params(("arbitrary",)),
        name="moe_gather",
    )(buf_tok.reshape(P // rb, 1, rb), h2)


_GU_TN = 2048
_PERM_W = 256


def _moe_gu_kernel(be_ref, xs_ref, w_ref, bg_ref, bl_ref, perm_ref, o_ref, w_scr):
    i = pl.program_id(1)
    prev = be_ref[jnp.maximum(i - 1, 0)]
    changed = jnp.logical_or(i == 0, be_ref[i] != prev)
    K = w_ref.shape[1]
    kr = 512

    @pl.when(changed)
    def _():
        pm = perm_ref[...]
        for r0 in range(0, K, kr):
            for g in range(_GU_TN // _PERM_W):
                wt = w_ref[0, r0:r0 + kr, g * _PERM_W:(g + 1) * _PERM_W].astype(BF16)
                w_scr[r0:r0 + kr, g * _PERM_W:(g + 1) * _PERM_W] = _dot(wt, pm).astype(BF16)

    gu = _dot(xs_ref[...], w_scr[...])
    hw = _PERM_W // 2
    for g in range(_GU_TN // _PERM_W):
        sl = slice(g * hw, (g + 1) * hw)
        xg = gu[:, g * _PERM_W:g * _PERM_W + hw] + bg_ref[0, :, sl]
        xl = gu[:, g * _PERM_W + hw:(g + 1) * _PERM_W] + bl_ref[0, :, sl]
        xg = jnp.minimum(xg, SWIGLU_LIMIT)
        xl = jnp.clip(xl, -SWIGLU_LIMIT, SWIGLU_LIMIT)
        act = xg * jax.nn.sigmoid(SWIGLU_ALPHA * xg) * (xl + 1.0)
        o_ref[:, sl] = act.astype(BF16)


def _moe_gu(block_expert, xs, w_gu, b_gate, b_lin, bm=MOE_BLOCK):
    P, D = xs.shape
    E, _, N2 = w_gu.shape
    F = N2 // 2
    nj = N2 // _GU_TN
    fo = _GU_TN // 2
    idx = np.arange(_PERM_W)
    src = np.where(idx < _PERM_W // 2, 2 * idx, 2 * (idx - _PERM_W // 2) + 1)
    pm = np.zeros((_PERM_W, _PERM_W), np.float32)
    pm[src, idx] = 1.0
    perm = jnp.asarray(pm, dtype=BF16)
    return pl.pallas_call(
        _moe_gu_kernel,
        out_shape=jax.ShapeDtypeStruct((P, F), BF16),
        grid_spec=pltpu.PrefetchScalarGridSpec(
            num_scalar_prefetch=1,
            grid=(nj, P // bm),
            in_specs=[
                pl.BlockSpec((bm, D), lambda j, i, be: (i, 0)),
                pl.BlockSpec((1, D, _GU_TN), lambda j, i, be: (be[i], 0, j)),
                pl.BlockSpec((1, 1, fo), lambda j, i, be: (be[i], 0, j)),
                pl.BlockSpec((1, 1, fo), lambda j, i, be: (be[i], 0, j)),
                pl.BlockSpec((_PERM_W, _PERM_W), lambda j, i, be: (0, 0)),
            ],
            out_specs=pl.BlockSpec((bm, fo), lambda j, i, be: (i, j)),
            scratch_shapes=[pltpu.VMEM((D, _GU_TN), BF16)],
        ),
        compiler_params=_params(("arbitrary", "arbitrary")),
        name="moe_gate_up",
    )(block_expert, xs, w_gu, b_gate, b_lin, perm)


def _moe_down_kernel(be_ref, h_ref, w_ref, b_ref, o_ref, w_scr):
    i = pl.program_id(0)
    prev = be_ref[jnp.maximum(i - 1, 0)]
    changed = jnp.logical_or(i == 0, be_ref[i] != prev)

    @pl.when(changed)
    def _():
        w_scr[...] = w_ref[0].astype(BF16)

    _store_slabs(o_ref, _dot(h_ref[...], w_scr[...]) + b_ref[0])


def _moe_down(block_expert, hact, w_down, b_down, bm=MOE_BLOCK):
    P, F = hact.shape
    E, _, D = w_down.shape
    return pl.pallas_call(
        _moe_down_kernel,
        out_shape=jax.ShapeDtypeStruct((P, D // LANES, LANES), F32),
        grid_spec=pltpu.PrefetchScalarGridSpec(
            num_scalar_prefetch=1,
            grid=(P // bm,),
            in_specs=[
                pl.BlockSpec((bm, F), lambda i, be: (i, 0)),
                pl.BlockSpec((1, F, D), lambda i, be: (be[i], 0, 0)),
                pl.BlockSpec((1, 1, D), lambda i, be: (be[i], 0, 0)),
            ],
            out_specs=pl.BlockSpec((bm, D // LANES, LANES), lambda i, be: (i, 0, 0)),
            scratch_shapes=[pltpu.VMEM((F, D), BF16)],
        ),
        compiler_params=_params(("arbitrary",)),
        name="moe_down",
    )(block_expert, hact, w_down, b_down)


def _combine_kernel(dest_ref, x1_ref, gate_ref, y_hbm, o_ref, buf, sem):
    tb = x1_ref.shape[0]

    def issue(a, carry):
        d = dest_ref[0, 0, a]
        pltpu.make_async_copy(y_hbm.at[d], buf.at[a], sem).start()
        return carry

    lax.fori_loop(0, TOP_K * tb, issue, 0)

    def drain(a, carry):
        pltpu.make_async_copy(y_hbm.at[0], buf.at[a], sem).wait()
        return carry

    lax.fori_loop(0, TOP_K * tb, drain, 0)
    acc = x1_ref[...]
    gates = gate_ref[...]
    for k in range(TOP_K):
        acc = acc + _load_slabs(buf, slice(k * tb, (k + 1) * tb)) * gates[:, k:k + 1]
    o_ref[...] = acc


def _combine(dest_km, x1, gates, y, tb=64):
    T, D = x1.shape
    return pl.pallas_call(
        _combine_kernel,
        out_shape=jax.ShapeDtypeStruct((T, D), F32),
        grid=(T // tb,),
        in_specs=[
            pl.BlockSpec((1, 1, TOP_K * tb), lambda i: (i, 0, 0), memory_space=pltpu.SMEM),
            pl.BlockSpec((tb, D), lambda i: (i, 0)),
            pl.BlockSpec((tb, LANES), lambda i: (i, 0)),
            pl.BlockSpec(memory_space=pl.ANY),
        ],
        out_specs=pl.BlockSpec((tb, D), lambda i: (i, 0)),
        scratch_shapes=[pltpu.VMEM((TOP_K * tb,) + y.shape[1:], F32), pltpu.SemaphoreType.DMA(())],
        compiler_params=_params(("arbitrary",)),
        name="moe_combine",
    )(dest_km, x1, gates, y)


def _t5_bucket(dist):
    max_exact = T5_BUCKETS // 2
    d = jnp.maximum(dist, 0)
    df = jnp.maximum(d, 1).astype(F32)
    large = max_exact + (jnp.log(df / max_exact) / math.log(T5_MAX_DISTANCE / max_exact)
                         * (T5_BUCKETS - max_exact)).astype(I32)
    large = jnp.minimum(large, T5_BUCKETS - 1)
    return jnp.where(d < max_exact, d, large)


def _toeplitz_bias(rel_t, R, C, offset, lo, hi):
    H = rel_t.shape[0]
    d = np.arange(R + C - 1) - (C - 1) + offset
    w = jnp.where(jnp.asarray((d >= lo) & (d < hi))[None], rel_t[:, _t5_bucket(jnp.asarray(d, I32))], NEG)
    m = R + C
    w_ext = jnp.pad(w[:, ::-1], ((0, 0), (0, 1)))
    y = jnp.tile(w_ext, (1, R))[:, :R * (m - 1)].reshape(H, R, m - 1)
    return y[:, :, R - 1:R - 1 + C].astype(F32)


def _pair_major(a, n_pair, n_half, n_c):
    sh = a.shape
    a = a.reshape((n_pair, n_half, n_c) + sh[1:])
    a = jnp.swapaxes(a, 1, 2)
    return a.reshape(sh)


def kernel(x, attn_norm_g, w_in, swa_q_gain, swa_k_gain, swa_sinks, moba_q_gain, moba_k_gain, rel_bias,
           w_out, ffn_norm_g, w_router, b_router, w_gate_up, b_gate_up, w_down, b_down):
    B, S, D = x.shape
    T = B * S
    assert w_in.shape[0] == 1, "single-layer kernel"
    qa_w = SWA_Q_HEADS * HEAD_DIM
    kv_w = SWA_KV_HEADS * HEAD_DIM
    mb_w = MOBA_HEADS * HEAD_DIM
    G = SWA_Q_HEADS // SWA_KV_HEADS
    npair = SWA_KV_HEADS // 2

    w0 = w_in[0]
    wq = w0[:, :qa_w].reshape(D, SWA_Q_HEADS, HEAD_DIM)
    wq = jnp.swapaxes(wq.reshape(D, npair, 2, G, HEAD_DIM), 2, 3).reshape(D, qa_w)
    w_in_bf = jnp.concatenate([wq, w0[:, qa_w:]], axis=1).astype(BF16)
    ones = jnp.ones((HEAD_DIM,), F32)
    tile = lambda v, n: jnp.tile(v.astype(F32), n)
    gain_col = jnp.concatenate([
        tile(swa_q_gain[0] * ATTN_SCALE, SWA_Q_HEADS), tile(swa_k_gain[0], SWA_KV_HEADS),
        tile(ones, SWA_KV_HEADS), tile(moba_q_gain[0] * ATTN_SCALE, MOBA_HEADS),
        tile(moba_k_gain[0], MOBA_HEADS), tile(ones, MOBA_HEADS)])[None, :]
    flag_np = np.concatenate([np.ones(qa_w + kv_w), np.zeros(kv_w), np.ones(2 * mb_w), np.zeros(mb_w)])
    flag_col = jnp.asarray(flag_np[None, :], F32)

    wo = w_out[0]
    wa = wo[:qa_w].reshape(npair, 2, G, HEAD_DIM, D)
    wa = jnp.swapaxes(wa, 1, 2).reshape(qa_w, D).astype(BF16)
    wb = wo[qa_w:].astype(BF16)

    rel_a = rel_bias[:, :SWA_Q_HEADS].T.astype(F32)
    rel_b = rel_bias[:, SWA_Q_HEADS:].T.astype(F32)
    W = SWA_WINDOW
    bias_a = _toeplitz_bias(rel_a, W, 2 * W, W, 0, W)
    bias_swa = _pair_major(bias_a, npair, 2, G).reshape(npair, G, 2 * W, 2 * W)
    sink_col = jnp.broadcast_to(swa_sinks[0].astype(F32)[:, None, None], (SWA_Q_HEADS, W, 1))
    sink_swa = _pair_major(sink_col, npair, 2, G).reshape(npair, G, 2 * W, 1)
    L = MOBA_BLOCK
    bias_own_t = jnp.swapaxes(_toeplitz_bias(rel_b, L, L, 0, 0, 2 * L), 1, 2)
    bias_adj_t = jnp.swapaxes(_toeplitz_bias(rel_b, L, L, L, 0, 2 * L), 1, 2)
    far = jnp.broadcast_to(rel_b[:, T5_BUCKETS - 1][:, None, None], (MOBA_HEADS, 1, LANES))

    x2 = x.reshape(T, D)
    proj = _inproj(x2, attn_norm_g[0][None, :].astype(F32), w_in_bf, gain_col, flag_col)
    proj3 = proj.reshape(B, S, proj.shape[1])
    kmean = _kmean(proj3)
    nblk = S // L
    assert nblk <= _GATE_SLOTS, "selection mask packs at most 16 key blocks per head"
    km = kmean.reshape(B, nblk, MOBA_HEADS, HEAD_DIM).transpose(0, 2, 3, 1)
    km = jnp.pad(km, ((0, 0), (0, 0), (0, 0), (0, _GATE_SLOTS - nblk)))
    eye = jnp.eye(MOBA_HEADS, dtype=F32)
    kmbd = (km[:, :, :, None, :] * eye[None, :, None, :, None]).reshape(B, mb_w, MOBA_HEADS * _GATE_SLOTS)
    kmbd_hi, kmbd_lo = _split_bf16(kmbd)
    mask = _select(proj3, kmbd_hi, kmbd_lo)
    oa = _swa(proj3, bias_swa, sink_swa)
    vbt = jnp.swapaxes(proj3[:, :, _VB * LANES:], 1, 2).reshape(B, MOBA_HEADS, HEAD_DIM, S)
    vbt = jnp.concatenate([vbt, jnp.ones((B, MOBA_HEADS, 1, S), BF16),
                           jnp.zeros((B, MOBA_HEADS, _VROWS - HEAD_DIM - 1, S), BF16)],
                          axis=2).reshape(B, MOBA_HEADS * _VROWS, S)
    ob = _moba(proj3, vbt, mask, bias_own_t, bias_adj_t, far)

    wr = jnp.pad(w_router[0].astype(F32), ((0, 0), (0, LANES - N_EXPERTS)))
    wr_hi, wr_lo = _split_bf16(wr)
    br = jnp.pad(b_router[0].astype(F32), (0, LANES - N_EXPERTS), constant_values=NEG)[None, :]
    x1, h2, idx_o, gate_o, rank_o, cnt_o = _outproj(
        x2, oa.reshape(T, qa_w), ob.reshape(T, mb_w), wa, wb, ffn_norm_g[0][None, :].astype(F32),
        wr_hi, wr_lo, br)

    A = T * TOP_K
    counts = cnt_o[0, :N_EXPERTS]
    padded = (counts + MOE_BLOCK - 1) // MOE_BLOCK * MOE_BLOCK
    pad_ends = jnp.cumsum(padded)
    pad_starts = pad_ends - padded
    n_blocks = -(-(A + N_EXPERTS * (MOE_BLOCK - 1)) // MOE_BLOCK)
    P = n_blocks * MOE_BLOCK
    top_idx = idx_o[:, :TOP_K]
    is_e = top_idx[:, :, None] == jnp.arange(N_EXPERTS, dtype=I32)
    dest = jnp.sum(jnp.where(is_e, pad_starts, 0), axis=-1) + rank_o[:, :TOP_K]
    tok = jnp.broadcast_to(jnp.arange(T, dtype=I32)[:, None], (T, TOP_K))
    buf_tok = jnp.zeros((P,), I32).at[dest.reshape(A)].set(tok.reshape(A))
    blk_start = jnp.arange(n_blocks, dtype=I32)[:, None] * MOE_BLOCK
    block_expert = jnp.minimum(jnp.sum((pad_ends[None, :] <= blk_start).astype(I32), axis=1),
                               N_EXPERTS - 1).astype(I32)

    xs = _gather_rows(buf_tok, h2)
    bgu = b_gate_up[0].astype(F32)
    b_gate = bgu[:, 0::2][:, None, :]
    b_lin = bgu[:, 1::2][:, None, :]
    hact = _moe_gu(block_expert, xs, w_gate_up[0], b_gate, b_lin)
    y = _moe_down(block_expert, hact, w_down[0], b_down[0].astype(F32)[:, None, :])
    tb = 64
    dest_km = dest.reshape(T // tb, tb, TOP_K).transpose(0, 2, 1).reshape(T // tb, 1, TOP_K * tb)
    out = _combine(dest_km, x1, gate_o, y, tb=tb)
    return out.reshape(B, S, D)
```

```python
import functools
import math

import numpy as np
import jax
import jax.numpy as jnp
from jax import lax
from jax.experimental import pallas as pl
from jax.experimental.pallas import tpu as pltpu

F32 = jnp.float32
BF16 = jnp.bfloat16
I32 = jnp.int32

HEAD_DIM = 64
SWA_Q_HEADS = 16
SWA_KV_HEADS = 4
SWA_WINDOW = 128
MOBA_HEADS = 16
MOBA_BLOCK = 256
MOBA_TOPK = 3
T5_BUCKETS = 32
T5_MAX_DISTANCE = 128
N_EXPERTS = 32
TOP_K = 4
SWIGLU_LIMIT = 7.0
SWIGLU_ALPHA = 1.702
MOE_BLOCK = 256
NORM_EPS = 1e-5
ATTN_SCALE = HEAD_DIM ** -0.5

LANES = 128
NEG = -1e30
VMEM_LIMIT = 56 * 1024 * 1024
_COMBINE_TB = 128
_GATE_SLOTS = 16
_HEADS_PER_MASK_BLOCK = LANES // _GATE_SLOTS

_QA, _KA, _VA, _QB, _KB, _VB = 0, 8, 10, 12, 20, 28


def _dot(a, b):
    return jnp.dot(a, b, preferred_element_type=F32)


def _dot_nt(a, b):
    return lax.dot_general(a, b, (((1,), (1,)), ((), ())), preferred_element_type=F32)


def _split_bf16(x):
    hi = x.astype(BF16)
    lo = (x - hi.astype(F32)).astype(BF16)
    return hi, lo


def _head_lane_masks(rows):
    lane = lax.broadcasted_iota(I32, (rows, LANES), 1)
    lo = jnp.where(lane < HEAD_DIM, 1.0, 0.0).astype(BF16)
    return lo, (1.0 - lo.astype(F32)).astype(BF16)


def _params(sem):
    return pltpu.CompilerParams(dimension_semantics=sem, vmem_limit_bytes=VMEM_LIMIT)


def _inproj_kernel(x_ref, g_ref, w_ref, gain_ref, flag_ref, bd_ref, o_ref, h_scr):
    @pl.when(pl.program_id(1) == 0)
    def _():
        x = x_ref[...]
        ms = jnp.mean(x * x, axis=-1, keepdims=True)
        h_scr[...] = (x * lax.rsqrt(ms + NORM_EPS) * g_ref[...]).astype(BF16)

    y = _dot(h_scr[...], w_ref[...])
    bd = bd_ref[...]
    for c in range(y.shape[1] // LANES):
        yc = y[:, c * LANES:(c + 1) * LANES]
        hi, lo = _split_bf16(yc * yc)
        ssum = _dot(hi, bd) + _dot(lo, bd)
        r = lax.rsqrt(ssum * (1.0 / HEAD_DIM) + NORM_EPS)
        sl = slice(c * LANES, (c + 1) * LANES)
        scale = jnp.where(flag_ref[:, sl] > 0.5, r, 1.0) * gain_ref[:, sl]
        o_ref[:, sl] = (yc * scale).astype(BF16)


def _inproj(x2, g, w_bf, gain_col, flag_col, tm=512, tn=1536):
    T, D = x2.shape
    N = w_bf.shape[1]
    blk = np.kron(np.eye(LANES // HEAD_DIM), np.ones((HEAD_DIM, HEAD_DIM))).astype(np.float32)
    bd = jnp.asarray(blk, dtype=BF16)
    return pl.pallas_call(
        _inproj_kernel,
        out_shape=jax.ShapeDtypeStruct((T, N), BF16),
        grid=(T // tm, N // tn),
        in_specs=[
            pl.BlockSpec((tm, D), lambda i, j: (i, 0)),
            pl.BlockSpec((1, D), lambda i, j: (0, 0)),
            pl.BlockSpec((D, tn), lambda i, j: (0, j)),
            pl.BlockSpec((1, tn), lambda i, j: (0, j)),
            pl.BlockSpec((1, tn), lambda i, j: (0, j)),
            pl.BlockSpec((LANES, LANES), lambda i, j: (0, 0)),
        ],
        out_specs=pl.BlockSpec((tm, tn), lambda i, j: (i, j)),
        scratch_shapes=[pltpu.VMEM((tm, D), BF16)],
        compiler_params=_params(("arbitrary", "arbitrary")),
        name="inproj",
    )(x2, g, w_bf, gain_col, flag_col, bd)


def _kmean_kernel(k_ref, o_ref):
    nblk = o_ref.shape[1]
    for j in range(nblk):
        kj = k_ref[0, j * MOBA_BLOCK:(j + 1) * MOBA_BLOCK, :].astype(F32)
        o_ref[0, j:j + 1, :] = jnp.sum(kj, axis=0, keepdims=True) * (1.0 / MOBA_BLOCK)


def _kmean(proj3):
    B, S, _ = proj3.shape
    nblk = S // MOBA_BLOCK
    W = MOBA_HEADS * HEAD_DIM
    return pl.pallas_call(
        _kmean_kernel,
        out_shape=jax.ShapeDtypeStruct((B, nblk, W), F32),
        grid=(B, W // 512),
        in_specs=[pl.BlockSpec((1, S, 512), lambda b, w: (b, 0, _KB * LANES // 512 + w))],
        out_specs=pl.BlockSpec((1, nblk, 512), lambda b, w: (b, 0, w)),
        compiler_params=_params(("arbitrary", "arbitrary")),
        name="kmean",
    )(proj3)


def _select_kernel(qlo_ref, qhi_ref, kmh_ref, kml_ref, o_ref):
    c = pl.program_id(1)
    half_k = qlo_ref.shape[2]
    qlo = qlo_ref[0]
    qhi = qhi_ref[0]
    g = (_dot(qlo, kmh_ref[0, :half_k, :]) + _dot(qhi, kmh_ref[0, half_k:, :])
         + _dot(qlo, kml_ref[0, :half_k, :]) + _dot(qhi, kml_ref[0, half_k:, :]))
    nb = _GATE_SLOTS
    lane = lax.broadcasted_iota(I32, (g.shape[0], LANES), 1)
    j = lane & (nb - 1)
    for half in range(g.shape[1] // LANES):
        gh = g[:, half * LANES:(half + 1) * LANES]
        cnt = jnp.zeros(gh.shape, F32)
        for d in range(1, nb):
            a = pltpu.roll(gh, LANES - d, 1)
            b = pltpu.roll(gh, nb - d, 1)
            wrap = (j + d) >= nb
            partner = jnp.where(wrap, b, a)
            jp = jnp.where(wrap, j + (d - nb), j + d)
            ahead = jnp.where(jp < j, jnp.where(partner >= gh, 1.0, 0.0), jnp.where(partner > gh, 1.0, 0.0))
            cnt = cnt + jnp.where(jp < c, ahead, 0.0)
        sel = jnp.where(j < c, cnt, 1e9) < (MOBA_TOPK - 0.5)
        o_ref[0, :, half * LANES:(half + 1) * LANES] = jnp.where(sel, 0.0, NEG).astype(BF16)


def _select(proj3, kmbd_hi, kmbd_lo):
    B, S, _ = proj3.shape
    nq = S // MOBA_BLOCK
    KW = MOBA_HEADS * HEAD_DIM
    NW = kmbd_hi.shape[2]
    qb0 = _QB * LANES // 512
    return pl.pallas_call(
        _select_kernel,
        out_shape=jax.ShapeDtypeStruct((B, S, NW), BF16),
        grid=(B, nq),
        in_specs=[
            pl.BlockSpec((1, MOBA_BLOCK, 512), lambda b, c: (b, c, qb0)),
            pl.BlockSpec((1, MOBA_BLOCK, 512), lambda b, c: (b, c, qb0 + 1)),
            pl.BlockSpec((1, KW, NW), lambda b, c: (b, 0, 0)),
            pl.BlockSpec((1, KW, NW), lambda b, c: (b, 0, 0)),
        ],
        out_specs=pl.BlockSpec((1, MOBA_BLOCK, NW), lambda b, c: (b, c, 0)),
        compiler_params=_params(("arbitrary", "arbitrary")),
        name="moba_select",
    )(proj3, proj3, kmbd_hi, kmbd_lo)


def _swa_kernel(q_ref, kc_ref, kp_ref, vc_ref, vp_ref, bias_ref, sink_ref, o_ref, k_scr, v_scr):
    W = SWA_WINDOW
    tq = q_ref.shape[1]
    first = pl.program_id(2) == 0
    k_scr[0:W, :] = kp_ref[0]
    k_scr[W:, :] = kc_ref[0]
    v_scr[0:W, :] = vp_ref[0]
    v_scr[W:, :] = vc_ref[0]
    lane = lax.broadcasted_iota(I32, (W, LANES), 1)
    keep_lo, keep_hi = _head_lane_masks(W)
    col = lax.broadcasted_iota(I32, (1, 2 * W), 1)
    nopast = jnp.where(jnp.logical_and(first, col < W), NEG, 0.0)
    for sb in range(tq // W):
        k2 = k_scr[sb * W:sb * W + 2 * W, :]
        v2 = v_scr[sb * W:sb * W + 2 * W, :]
        for c in range(q_ref.shape[2] // LANES):
            qc = q_ref[0, sb * W:(sb + 1) * W, c * LANES:(c + 1) * LANES]
            qab = jnp.concatenate([qc * keep_lo, qc * keep_hi], axis=0)
            s = _dot_nt(qab, k2) + bias_ref[0, c]
            if sb == 0:
                s = s + nopast
            sink = sink_ref[0, c]
            m = jnp.maximum(jnp.max(s, axis=-1, keepdims=True), sink)
            p = jnp.exp(s - m)
            l = jnp.sum(p, axis=-1, keepdims=True) + jnp.exp(sink - m)
            o = _dot(p.astype(BF16), v2) / l
            o_ref[0, sb * W:(sb + 1) * W, c * LANES:(c + 1) * LANES] = jnp.where(
                lane < HEAD_DIM, o[:W], o[W:]).astype(BF16)


def _swa(proj3, bias_swa, sink_swa, tq=512):
    B, S, _ = proj3.shape
    W = SWA_WINDOW
    r = tq // W
    npair = SWA_KV_HEADS // 2
    qw = SWA_Q_HEADS * HEAD_DIM // npair
    nc = qw // LANES
    return pl.pallas_call(
        _swa_kernel,
        out_shape=jax.ShapeDtypeStruct((B, S, SWA_Q_HEADS * HEAD_DIM), BF16),
        grid=(B, npair, S // tq),
        in_specs=[
            pl.BlockSpec((1, tq, qw), lambda b, p, i: (b, i, p)),
            pl.BlockSpec((1, tq, LANES), lambda b, p, i: (b, i, _KA + p)),
            pl.BlockSpec((1, W, LANES), lambda b, p, i: (b, jnp.maximum(i * r - 1, 0), _KA + p)),
            pl.BlockSpec((1, tq, LANES), lambda b, p, i: (b, i, _VA + p)),
            pl.BlockSpec((1, W, LANES), lambda b, p, i: (b, jnp.maximum(i * r - 1, 0), _VA + p)),
            pl.BlockSpec((1, nc, 2 * W, 2 * W), lambda b, p, i: (p, 0, 0, 0)),
            pl.BlockSpec((1, nc, 2 * W, 1), lambda b, p, i: (p, 0, 0, 0)),
        ],
        out_specs=pl.BlockSpec((1, tq, qw), lambda b, p, i: (b, i, p)),
        scratch_shapes=[pltpu.VMEM((tq + W, LANES), BF16), pltpu.VMEM((tq + W, LANES), BF16)],
        compiler_params=_params(("arbitrary", "arbitrary", "arbitrary")),
        name="swa_attn",
    )(proj3, proj3, proj3, proj3, proj3, bias_swa, sink_swa)


_MOBA_HG = _HEADS_PER_MASK_BLOCK


_VROWS = 80


def _moba_kernel(q_ref, k_ref, vt_ref, m_ref, bo_ref, ba_ref, far_ref, o_ref,
                 qaug_scr, sa_scr, sb_scr, m_scr, acc_scr):
    L = MOBA_BLOCK
    c = pl.program_id(2)
    lane = lax.broadcasted_iota(I32, (L, LANES), 1)
    keep = _head_lane_masks(L)
    mblk = m_ref[0]

    def rows(j):
        return pl.ds(pl.multiple_of(j * L, L), L)

    def pair(hl):
        return slice((hl // 2) * LANES, (hl // 2 + 1) * LANES)

    for hl in range(_MOBA_HG):
        qaug_scr[hl, :, 0:LANES] = q_ref[0, :, pair(hl)] * keep[hl % 2]
        qaug_scr[hl, :, LANES:2 * LANES] = mblk
    m_scr[...] = jnp.full(m_scr.shape, NEG, F32)
    acc_scr[...] = jnp.zeros(acc_scr.shape, F32)

    def scores(hl, j, s_buf, masked=True):
        kj = k_ref[0, rows(j), pair(hl)]
        if masked:
            onehot = jnp.where(lane == hl * _GATE_SLOTS + j, 1.0, 0.0).astype(BF16)
            s_buf[hl] = _dot_nt(jnp.concatenate([kj, onehot], axis=1), qaug_scr[hl])
        else:
            s_buf[hl] = _dot_nt(kj, qaug_scr[hl, :, 0:LANES])

    def softmax_pv(hl, j, s_buf, kind):
        s = s_buf[hl]
        cb = 0.0
        if kind == "own":
            s = s + bo_ref[hl]
        elif kind == "adj":
            s = s + ba_ref[hl]
        else:
            cb = far_ref[hl, :, 0:1]
        m_old = m_scr[hl]
        m_new = jnp.maximum(m_old, jnp.max(s, axis=0, keepdims=True) + cb)
        pr = jnp.exp(s - (m_new - cb)).astype(BF16)
        m_scr[hl] = m_new
        vt = vt_ref[0, hl * _VROWS:(hl + 1) * _VROWS, rows(j)]
        acc_scr[hl] = acc_scr[hl] * jnp.exp(m_old - m_new) + _dot(vt, pr)

    def stage(qk, sm):
        for hl in range(_MOBA_HG):
            if qk is not None:
                scores(hl, *qk)
            if sm is not None:
                softmax_pv(hl, *sm)

    n_far = jnp.maximum(c - 1, 0)
    n_pair = (n_far + 1) // 2

    @pl.when(c >= 1)
    def _():
        stage((0, sa_scr), None)

    def far_body(tt, carry):
        t0 = 2 * tt
        j1 = jnp.where(t0 + 1 < n_far, t0 + 1, c)
        stage((j1, sb_scr), (t0, sa_scr, "far"))
        stage((jnp.minimum(t0 + 2, c - 1), sa_scr), (j1, sb_scr, "far"))
        return carry

    lax.fori_loop(0, n_pair, far_body, 0)

    @pl.when(c >= 1)
    def _():
        stage((c, sb_scr, False), (c - 1, sa_scr, "adj"))

    @pl.when(c == 0)
    def _():
        stage((c, sb_scr, False), None)

    stage(None, (c, sb_scr, "own"))

    for pr_ in range(_MOBA_HG // 2):
        a0 = acc_scr[2 * pr_]
        a1 = acc_scr[2 * pr_ + 1]
        ot = jnp.concatenate([a0[:HEAD_DIM] / a0[HEAD_DIM:HEAD_DIM + 1],
                              a1[:HEAD_DIM] / a1[HEAD_DIM:HEAD_DIM + 1]], axis=0)
        o_ref[0, :, pr_ * LANES:(pr_ + 1) * LANES] = ot.T.astype(BF16)


def _moba(proj3, vbt, mask, bias_own_t, bias_adj_t, far):
    B, S, _ = proj3.shape
    L = MOBA_BLOCK
    hg = _MOBA_HG
    gw = hg * HEAD_DIM
    ng = MOBA_HEADS // hg
    return pl.pallas_call(
        _moba_kernel,
        out_shape=jax.ShapeDtypeStruct((B, S, MOBA_HEADS * HEAD_DIM), BF16),
        grid=(B, ng, S // L),
        in_specs=[
            pl.BlockSpec((1, L, gw), lambda b, g, c: (b, c, _QB * LANES // gw + g)),
            pl.BlockSpec((1, S, gw), lambda b, g, c: (b, 0, _KB * LANES // gw + g)),
            pl.BlockSpec((1, hg * _VROWS, S), lambda b, g, c: (b, g, 0)),
            pl.BlockSpec((1, L, LANES), lambda b, g, c: (b, c, g)),
            pl.BlockSpec((hg, L, L), lambda b, g, c: (g, 0, 0)),
            pl.BlockSpec((hg, L, L), lambda b, g, c: (g, 0, 0)),
            pl.BlockSpec((hg, 1, LANES), lambda b, g, c: (g, 0, 0)),
        ],
        out_specs=pl.BlockSpec((1, L, gw), lambda b, g, c: (b, c, g)),
        scratch_shapes=[pltpu.VMEM((hg, L, 2 * LANES), BF16), pltpu.VMEM((hg, L, L), F32),
                        pltpu.VMEM((hg, L, L), F32), pltpu.VMEM((hg, 1, L), F32),
                        pltpu.VMEM((hg, _VROWS, L), F32)],
        compiler_params=_params(("arbitrary", "arbitrary", "arbitrary")),
        name="moba_attn",
    )(proj3, proj3, vbt, mask, bias_own_t, bias_adj_t, far)


def _outproj_kernel(x_ref, oa_ref, ob_ref, wa_ref, wb_ref, g_ref, wrh_ref, wrl_ref, br_ref,
                    x1_ref, h2_ref, idx_ref, gate_ref, rank_ref, cnt_ref, run_scr):
    i = pl.program_id(0)

    @pl.when(i == 0)
    def _():
        run_scr[...] = jnp.zeros_like(run_scr)

    x1 = x_ref[...] + _dot(oa_ref[...], wa_ref[...]) + _dot(ob_ref[...], wb_ref[...])
    x1_ref[...] = x1
    ms = jnp.mean(x1 * x1, axis=-1, keepdims=True)
    h2 = x1 * lax.rsqrt(ms + NORM_EPS) * g_ref[...]
    h2_ref[...] = h2
    hh, hl = _split_bf16(h2)
    logits = (_dot(hh, wrh_ref[...]) + _dot(hl, wrh_ref[...]) + _dot(hh, wrl_ref[...])) + br_ref[...]

    tm = logits.shape[0]
    lane = lax.broadcasted_iota(I32, (tm, LANES), 1)
    vals, idxs = [], []
    for _k in range(TOP_K):
        mx = jnp.max(logits, axis=-1, keepdims=True)
        ix = jnp.min(jnp.where(logits == mx, lane, LANES), axis=-1, keepdims=True)
        vals.append(mx)
        idxs.append(ix)
        logits = jnp.where(lane == ix, -3e38, logits)
    es = [jnp.exp(v - vals[0]) for v in vals]
    den = es[0] + es[1] + es[2] + es[3]

    onehots = [lane == ix for ix in idxs]
    ohsum = jnp.zeros((tm, LANES), F32)
    for oh in onehots:
        ohsum = ohsum + jnp.where(oh, 1.0, 0.0)
    r_i = lax.broadcasted_iota(I32, (tm, tm), 0)
    c_i = lax.broadcasted_iota(I32, (tm, tm), 1)
    lower = jnp.where(r_i > c_i, 1.0, 0.0).astype(BF16)
    base = run_scr[0:1, :] + _dot(lower, ohsum.astype(BF16))

    idx_out = jnp.zeros((tm, LANES), I32)
    gate_out = jnp.zeros((tm, LANES), F32)
    rank_out = jnp.zeros((tm, LANES), F32)
    for k in range(TOP_K):
        rk = jnp.sum(jnp.where(onehots[k], base, 0.0), axis=-1, keepdims=True)
        idx_out = jnp.where(lane == k, idxs[k], idx_out)
        gate_out = jnp.where(lane == k, es[k] / den, gate_out)
        rank_out = jnp.where(lane == k, rk, rank_out)
    idx_ref[...] = idx_out
    gate_ref[...] = gate_out
    rank_ref[...] = rank_out.astype(I32)
    run = run_scr[0:1, :] + jnp.sum(ohsum, axis=0, keepdims=True)
    run_scr[...] = jnp.broadcast_to(run, run_scr.shape)
    cnt_ref[...] = jnp.broadcast_to(run, cnt_ref.shape).astype(I32)


def _outproj(x2, oa, ob, wa, wb, g, wr_hi, wr_lo, br, tm=256):
    T, D = x2.shape
    Ka = oa.shape[1]
    Kb = ob.shape[1]
    row = lambda i: (i, 0)
    fix = lambda i: (0, 0)
    return pl.pallas_call(
        _outproj_kernel,
        out_shape=(
            jax.ShapeDtypeStruct((T, D), F32),
            jax.ShapeDtypeStruct((T, D), F32),
            jax.ShapeDtypeStruct((T, LANES), I32),
            jax.ShapeDtypeStruct((T, LANES), F32),
            jax.ShapeDtypeStruct((T, LANES), I32),
            jax.ShapeDtypeStruct((8, LANES), I32),
        ),
        grid=(T // tm,),
        in_specs=[
            pl.BlockSpec((tm, D), row),
            pl.BlockSpec((tm, Ka), row),
            pl.BlockSpec((tm, Kb), row),
            pl.BlockSpec((Ka, D), fix),
            pl.BlockSpec((Kb, D), fix),
            pl.BlockSpec((1, D), fix),
            pl.BlockSpec((D, LANES), fix),
            pl.BlockSpec((D, LANES), fix),
            pl.BlockSpec((1, LANES), fix),
        ],
        out_specs=(
            pl.BlockSpec((tm, D), row),
            pl.BlockSpec((tm, D), row),
            pl.BlockSpec((tm, LANES), row),
            pl.BlockSpec((tm, LANES), row),
            pl.BlockSpec((tm, LANES), row),
            pl.BlockSpec((8, LANES), fix),
        ),
        scratch_shapes=[pltpu.VMEM((8, LANES), F32)],
        compiler_params=_params(("arbitrary",)),
        name="outproj_router",
    )(x2, oa, ob, wa, wb, g, wr_hi, wr_lo, br)


_PERM_W = 256


def _prep_gu_kernel(w_ref, perm_ref, o_ref):
    pm = perm_ref[...]
    F = o_ref.shape[2] // 2
    hw = _PERM_W // 2
    for g in range(w_ref.shape[2] // _PERM_W):
        wt = w_ref[0, :, g * _PERM_W:(g + 1) * _PERM_W].astype(BF16)
        d = _dot(wt, pm).astype(BF16)
        o_ref[0, :, g * hw:(g + 1) * hw] = d[:, :hw]
        o_ref[0, :, F + g * hw:F + (g + 1) * hw] = d[:, hw:]


def _prep_gu(w_gu, tk=512):
    E, K, N2 = w_gu.shape
    idx = np.arange(_PERM_W)
    src = np.where(idx < _PERM_W // 2, 2 * idx, 2 * (idx - _PERM_W // 2) + 1)
    pm = np.zeros((_PERM_W, _PERM_W), np.float32)
    pm[src, idx] = 1.0
    return pl.pallas_call(
        _prep_gu_kernel,
        out_shape=jax.ShapeDtypeStruct((E, K, N2), BF16),
        grid=(E, K // tk),
        in_specs=[pl.BlockSpec((1, tk, N2), lambda e, k: (e, k, 0)),
                  pl.BlockSpec((_PERM_W, _PERM_W), lambda e, k: (0, 0))],
        out_specs=pl.BlockSpec((1, tk, N2), lambda e, k: (e, k, 0)),
        compiler_params=_params(("arbitrary", "arbitrary")),
        name="moe_prep_gate_up",
    )(w_gu, jnp.asarray(pm, dtype=BF16))


def _issue_row_gather(idx_ref, src_hbm, dst, sem):
    def body(r, carry):
        pltpu.make_async_copy(src_hbm.at[pl.ds(idx_ref[0, 0, r], 1)], dst.at[pl.ds(r, 1)], sem).start()
        return carry

    lax.fori_loop(0, dst.shape[0], body, 0)


def _moe_gu_kernel(be_ref, tokc_ref, tokn_ref, h_hbm, w_ref, bg_ref, bl_ref, o_ref, xbuf, sem):
    i = pl.program_id(0)
    slot = i % 2
    bm = xbuf.shape[1]

    @pl.when(i == 0)
    def _():
        _issue_row_gather(tokc_ref, h_hbm, xbuf.at[0], sem.at[0])

    @pl.when(i + 1 < pl.num_programs(0))
    def _():
        _issue_row_gather(tokn_ref, h_hbm, xbuf.at[1 - slot], sem.at[1 - slot])

    pltpu.make_async_copy(h_hbm.at[pl.ds(0, bm)], xbuf.at[slot], sem.at[slot]).wait()
    x = xbuf[slot].astype(BF16)
    F = o_ref.shape[1]
    nh = 2
    fh = F // nh
    for h in range(nh):
        xg = _dot(x, w_ref[0, :, h * fh:(h + 1) * fh]) + bg_ref[0, :, h * fh:(h + 1) * fh]
        xl = _dot(x, w_ref[0, :, F + h * fh:F + (h + 1) * fh]) + bl_ref[0, :, h * fh:(h + 1) * fh]
        xg = jnp.minimum(xg, SWIGLU_LIMIT)
        xl = jnp.clip(xl, -SWIGLU_LIMIT, SWIGLU_LIMIT)
        act = xg * jax.nn.sigmoid(SWIGLU_ALPHA * xg) * (xl + 1.0)
        o_ref[:, h * fh:(h + 1) * fh] = act.astype(BF16)


def _moe_gu(block_expert, buf_tok, h2, w_gu_bf, b_gate, b_lin, bm=MOE_BLOCK):
    P = buf_tok.shape[0]
    T, D = h2.shape
    E, _, N2 = w_gu_bf.shape
    F = N2 // 2
    nb = P // bm
    tok3 = buf_tok.reshape(nb, 1, bm)
    return pl.pallas_call(
        _moe_gu_kernel,
        out_shape=jax.ShapeDtypeStruct((P, F), BF16),
        grid_spec=pltpu.PrefetchScalarGridSpec(
            num_scalar_prefetch=1,
            grid=(nb,),
            in_specs=[
                pl.BlockSpec((1, 1, bm), lambda i, be: (i, 0, 0), memory_space=pltpu.SMEM),
                pl.BlockSpec((1, 1, bm), lambda i, be: (jnp.minimum(i + 1, nb - 1), 0, 0),
                             memory_space=pltpu.SMEM),
                pl.BlockSpec(memory_space=pl.ANY),
                pl.BlockSpec((1, D, N2), lambda i, be: (be[i], 0, 0)),
                pl.BlockSpec((1, 1, F), lambda i, be: (be[i], 0, 0)),
                pl.BlockSpec((1, 1, F), lambda i, be: (be[i], 0, 0)),
            ],
            out_specs=pl.BlockSpec((bm, F), lambda i, be: (i, 0)),
            scratch_shapes=[pltpu.VMEM((2, bm, D), F32), pltpu.SemaphoreType.DMA((2,))],
        ),
        compiler_params=_params(("arbitrary",)),
        name="moe_gather_gate_up",
    )(block_expert, tok3, tok3, h2, w_gu_bf, b_gate, b_lin)


def _moe_down_kernel(be_ref, nv_ref, inv_ref, h_ref, w_ref, b_ref, yt_hbm, ybuf, sem, w_scr):
    i = pl.program_id(0)
    n = pl.num_programs(0)
    slot = i % 2
    bm = ybuf.shape[1]

    def drain(step, s):
        def body(r, carry):
            pltpu.make_async_copy(ybuf.at[s, pl.ds(0, 1)], yt_hbm.at[pl.ds(0, 1)], sem.at[s]).wait()
            return carry

        lax.fori_loop(0, nv_ref[step], body, 0)

    @pl.when(i >= 2)
    def _():
        drain(i - 2, slot)

    prev = be_ref[jnp.maximum(i - 1, 0)]

    @pl.when(jnp.logical_or(i == 0, be_ref[i] != prev))
    def _():
        w_scr[...] = w_ref[0].astype(BF16)

    ybuf[slot] = _dot(h_ref[...], w_scr[...]) + b_ref[0]

    def issue(r, carry):
        pltpu.make_async_copy(ybuf.at[slot, pl.ds(r, 1)], yt_hbm.at[pl.ds(inv_ref[0, 0, r], 1)],
                              sem.at[slot]).start()
        return carry

    lax.fori_loop(0, nv_ref[i], issue, 0)

    @pl.when(i == n - 1)
    def _():
        @pl.when(n >= 2)
        def _():
            drain(i - 1, 1 - slot)
        drain(i, slot)


def _moe_down(block_expert, n_valid, inv, hact, w_down, b_down, n_rows, bm=MOE_BLOCK):
    P, F = hact.shape
    E, _, D = w_down.shape
    nb = P // bm
    return pl.pallas_call(
        _moe_down_kernel,
        out_shape=jax.ShapeDtypeStruct((n_rows, D), F32),
        grid_spec=pltpu.PrefetchScalarGridSpec(
            num_scalar_prefetch=2,
            grid=(nb,),
            in_specs=[
                pl.BlockSpec((1, 1, bm), lambda i, be, nv: (i, 0, 0), memory_space=pltpu.SMEM),
                pl.BlockSpec((bm, F), lambda i, be, nv: (i, 0)),
                pl.BlockSpec((1, F, D), lambda i, be, nv: (be[i], 0, 0)),
                pl.BlockSpec((1, 1, D), lambda i, be, nv: (be[i], 0, 0)),
            ],
            out_specs=pl.BlockSpec(memory_space=pl.ANY),
            scratch_shapes=[pltpu.VMEM((2, bm, D), F32), pltpu.SemaphoreType.DMA((2,)),
                            pltpu.VMEM((F, D), BF16)],
        ),
        compiler_params=_params(("arbitrary",)),
        name="moe_down_scatter",
    )(block_expert, n_valid, inv.reshape(nb, 1, bm), hact, w_down, b_down)


def _combine_kernel(x1_ref, gate_ref, y_ref, o_ref):
    tb = x1_ref.shape[0]
    acc = x1_ref[...]
    gates = gate_ref[...]
    for k in range(TOP_K):
        acc = acc + y_ref[k * tb:(k + 1) * tb, :] * gates[:, k:k + 1]
    o_ref[...] = acc


def _combine(x1, gates, yt, tb):
    T, D = x1.shape
    return pl.pallas_call(
        _combine_kernel,
        out_shape=jax.ShapeDtypeStruct((T, D), F32),
        grid=(T // tb,),
        in_specs=[
            pl.BlockSpec((tb, D), lambda i: (i, 0)),
            pl.BlockSpec((tb, LANES), lambda i: (i, 0)),
            pl.BlockSpec((TOP_K * tb, D), lambda i: (i, 0)),
        ],
        out_specs=pl.BlockSpec((tb, D), lambda i: (i, 0)),
        compiler_params=_params(("arbitrary",)),
        name="moe_combine",
    )(x1, gates, yt)


def _t5_bucket(dist):
    max_exact = T5_BUCKETS // 2
    d = jnp.maximum(dist, 0)
    df = jnp.maximum(d, 1).astype(F32)
    large = max_exact + (jnp.log(df / max_exact) / math.log(T5_MAX_DISTANCE / max_exact)
                         * (T5_BUCKETS - max_exact)).astype(I32)
    large = jnp.minimum(large, T5_BUCKETS - 1)
    return jnp.where(d < max_exact, d, large)


def _toeplitz_bias(rel_t, R, C, offset, lo, hi):
    H = rel_t.shape[0]
    d = np.arange(R + C - 1) - (C - 1) + offset
    w = jnp.where(jnp.asarray((d >= lo) & (d < hi))[None], rel_t[:, _t5_bucket(jnp.asarray(d, I32))], NEG)
    m = R + C
    w_ext = jnp.pad(w[:, ::-1], ((0, 0), (0, 1)))
    y = jnp.tile(w_ext, (1, R))[:, :R * (m - 1)].reshape(H, R, m - 1)
    return y[:, :, R - 1:R - 1 + C].astype(F32)


def _pair_major(a, n_pair, n_half, n_c):
    sh = a.shape
    a = a.reshape((n_pair, n_half, n_c) + sh[1:])
    a = jnp.swapaxes(a, 1, 2)
    return a.reshape(sh)


def kernel(x, attn_norm_g, w_in, swa_q_gain, swa_k_gain, swa_sinks, moba_q_gain, moba_k_gain, rel_bias,
           w_out, ffn_norm_g, w_router, b_router, w_gate_up, b_gate_up, w_down, b_down):
    B, S, D = x.shape
    T = B * S
    assert w_in.shape[0] == 1, "single-layer kernel"
    qa_w = SWA_Q_HEADS * HEAD_DIM
    kv_w = SWA_KV_HEADS * HEAD_DIM
    mb_w = MOBA_HEADS * HEAD_DIM
    G = SWA_Q_HEADS // SWA_KV_HEADS
    npair = SWA_KV_HEADS // 2

    w0 = w_in[0]
    wq = w0[:, :qa_w].reshape(D, SWA_Q_HEADS, HEAD_DIM)
    wq = jnp.swapaxes(wq.reshape(D, npair, 2, G, HEAD_DIM), 2, 3).reshape(D, qa_w)
    w_in_bf = jnp.concatenate([wq, w0[:, qa_w:]], axis=1).astype(BF16)
    ones = jnp.ones((HEAD_DIM,), F32)
    tile = lambda v, n: jnp.tile(v.astype(F32), n)
    gain_col = jnp.concatenate([
        tile(swa_q_gain[0] * ATTN_SCALE, SWA_Q_HEADS), tile(swa_k_gain[0], SWA_KV_HEADS),
        tile(ones, SWA_KV_HEADS), tile(moba_q_gain[0] * ATTN_SCALE, MOBA_HEADS),
        tile(moba_k_gain[0], MOBA_HEADS), tile(ones, MOBA_HEADS)])[None, :]
    flag_np = np.concatenate([np.ones(qa_w + kv_w), np.zeros(kv_w), np.ones(2 * mb_w), np.zeros(mb_w)])
    flag_col = jnp.asarray(flag_np[None, :], F32)

    wo = w_out[0]
    wa = wo[:qa_w].reshape(npair, 2, G, HEAD_DIM, D)
    wa = jnp.swapaxes(wa, 1, 2).reshape(qa_w, D).astype(BF16)
    wb = wo[qa_w:].astype(BF16)

    rel_a = rel_bias[:, :SWA_Q_HEADS].T.astype(F32)
    rel_b = rel_bias[:, SWA_Q_HEADS:].T.astype(F32)
    W = SWA_WINDOW
    bias_a = _toeplitz_bias(rel_a, W, 2 * W, W, 0, W)
    bias_swa = _pair_major(bias_a, npair, 2, G).reshape(npair, G, 2 * W, 2 * W)
    sink_col = jnp.broadcast_to(swa_sinks[0].astype(F32)[:, None, None], (SWA_Q_HEADS, W, 1))
    sink_swa = _pair_major(sink_col, npair, 2, G).reshape(npair, G, 2 * W, 1)
    L = MOBA_BLOCK
    bias_own_t = jnp.swapaxes(_toeplitz_bias(rel_b, L, L, 0, 0, 2 * L), 1, 2)
    bias_adj_t = jnp.swapaxes(_toeplitz_bias(rel_b, L, L, L, 0, 2 * L), 1, 2)
    far = jnp.broadcast_to(rel_b[:, T5_BUCKETS - 1][:, None, None], (MOBA_HEADS, 1, LANES))

    x2 = x.reshape(T, D)
    proj = _inproj(x2, attn_norm_g[0][None, :].astype(F32), w_in_bf, gain_col, flag_col)
    proj3 = proj.reshape(B, S, proj.shape[1])
    kmean = _kmean(proj3)
    nblk = S // L
    assert nblk <= _GATE_SLOTS, "selection mask packs at most 16 key blocks per head"
    km = kmean.reshape(B, nblk, MOBA_HEADS, HEAD_DIM).transpose(0, 2, 3, 1)
    km = jnp.pad(km, ((0, 0), (0, 0), (0, 0), (0, _GATE_SLOTS - nblk)))
    eye = jnp.eye(MOBA_HEADS, dtype=F32)
    kmbd = (km[:, :, :, None, :] * eye[None, :, None, :, None]).reshape(B, mb_w, MOBA_HEADS * _GATE_SLOTS)
    kmbd_hi, kmbd_lo = _split_bf16(kmbd)
    mask = _select(proj3, kmbd_hi, kmbd_lo)
    oa = _swa(proj3, bias_swa, sink_swa)
    vbt = jnp.swapaxes(proj3[:, :, _VB * LANES:], 1, 2).reshape(B, MOBA_HEADS, HEAD_DIM, S)
    vbt = jnp.concatenate([vbt, jnp.ones((B, MOBA_HEADS, 1, S), BF16),
                           jnp.zeros((B, MOBA_HEADS, _VROWS - HEAD_DIM - 1, S), BF16)],
                          axis=2).reshape(B, MOBA_HEADS * _VROWS, S)
    ob = _moba(proj3, vbt, mask, bias_own_t, bias_adj_t, far)

    wr = jnp.pad(w_router[0].astype(F32), ((0, 0), (0, LANES - N_EXPERTS)))
    wr_hi, wr_lo = _split_bf16(wr)
    br = jnp.pad(b_router[0].astype(F32), (0, LANES - N_EXPERTS), constant_values=NEG)[None, :]
    x1, h2, idx_o, gate_o, rank_o, cnt_o = _outproj(
        x2, oa.reshape(T, qa_w), ob.reshape(T, mb_w), wa, wb, ffn_norm_g[0][None, :].astype(F32),
        wr_hi, wr_lo, br)

    A = T * TOP_K
    counts = cnt_o[0, :N_EXPERTS]
    padded = (counts + MOE_BLOCK - 1) // MOE_BLOCK * MOE_BLOCK
    pad_ends = jnp.cumsum(padded)
    pad_starts = pad_ends - padded
    n_blocks = -(-(A + N_EXPERTS * (MOE_BLOCK - 1)) // MOE_BLOCK)
    P = n_blocks * MOE_BLOCK
    top_idx = idx_o[:, :TOP_K]
    is_e = top_idx[:, :, None] == jnp.arange(N_EXPERTS, dtype=I32)
    dest = jnp.sum(jnp.where(is_e, pad_starts, 0), axis=-1) + rank_o[:, :TOP_K]
    tb = _COMBINE_TB
    t_idx = jnp.arange(T, dtype=I32)[:, None]
    out_row = (t_idx // tb) * (TOP_K * tb) + jnp.arange(TOP_K, dtype=I32)[None, :] * tb + t_idx % tb
    inv = jnp.full((P,), -1, I32).at[dest.reshape(A)].set(out_row.reshape(A))
    inv_c = jnp.maximum(inv, 0)
    buf_tok = jnp.where(inv >= 0, (inv_c // (TOP_K * tb)) * tb + inv_c % tb, 0)
    blk_start = jnp.arange(n_blocks, dtype=I32)[:, None] * MOE_BLOCK
    block_expert = jnp.minimum(jnp.sum((pad_ends[None, :] <= blk_start).astype(I32), axis=1),
                               N_EXPERTS - 1).astype(I32)
    is_be = block_expert[:, None] == jnp.arange(N_EXPERTS, dtype=I32)[None, :]
    used = blk_start[:, 0] - jnp.sum(jnp.where(is_be, pad_starts[None, :], 0), axis=1)
    n_valid = jnp.clip(jnp.sum(jnp.where(is_be, counts[None, :], 0), axis=1) - used, 0, MOE_BLOCK).astype(I32)

    bgu = b_gate_up[0].astype(F32).reshape(N_EXPERTS, -1, 2)
    b_gate = bgu[:, :, 0][:, None, :]
    b_lin = bgu[:, :, 1][:, None, :]
    w_gu_bf = _prep_gu(w_gate_up[0])
    hact = _moe_gu(block_expert, buf_tok, h2, w_gu_bf, b_gate, b_lin)
    yt = _moe_down(block_expert, n_valid, inv_c, hact, w_down[0], b_down[0].astype(F32)[:, None, :], A)
    out = _combine(x1, gate_o, yt, tb)
    return out.reshape(B, S, D)
```

```python
import functools
import math

import numpy as np
import jax
import jax.numpy as jnp
from jax import lax
from jax.experimental import pallas as pl
from jax.experimental.pallas import tpu as pltpu

F32 = jnp.float32
BF16 = jnp.bfloat16
I32 = jnp.int32

HEAD_DIM = 64
SWA_Q_HEADS = 16
SWA_KV_HEADS = 4
SWA_WINDOW = 128
MOBA_HEADS = 16
MOBA_BLOCK = 256
MOBA_TOPK = 3
T5_BUCKETS = 32
T5_MAX_DISTANCE = 128
N_EXPERTS = 32
TOP_K = 4
SWIGLU_LIMIT = 7.0
SWIGLU_ALPHA = 1.702
MOE_BLOCK = 256
NORM_EPS = 1e-5
ATTN_SCALE = HEAD_DIM ** -0.5

LANES = 128
NEG = -1e30
VMEM_LIMIT = 56 * 1024 * 1024
_COMBINE_TB = 128
_GATE_SLOTS = 16
_HEADS_PER_MASK_BLOCK = LANES // _GATE_SLOTS

_QA, _KA, _VA, _QB, _KB, _VB = 0, 8, 10, 12, 20, 28


def _dot(a, b):
    return jnp.dot(a, b, preferred_element_type=F32)


def _dot_nt(a, b):
    return lax.dot_general(a, b, (((1,), (1,)), ((), ())), preferred_element_type=F32)


def _split_bf16(x):
    hi = x.astype(BF16)
    lo = (x - hi.astype(F32)).astype(BF16)
    return hi, lo


def _head_lane_masks(rows):
    lane = lax.broadcasted_iota(I32, (rows, LANES), 1)
    lo = jnp.where(lane < HEAD_DIM, 1.0, 0.0).astype(BF16)
    return lo, (1.0 - lo.astype(F32)).astype(BF16)


def _params(sem):
    return pltpu.CompilerParams(dimension_semantics=sem, vmem_limit_bytes=VMEM_LIMIT)


def _inproj_kernel(x_ref, g_ref, w_ref, gain_ref, flag_ref, bd_ref, o_ref, h_scr):
    @pl.when(pl.program_id(1) == 0)
    def _():
        x = x_ref[...]
        ms = jnp.mean(x * x, axis=-1, keepdims=True)
        h_scr[...] = (x * lax.rsqrt(ms + NORM_EPS) * g_ref[...]).astype(BF16)

    y = _dot(h_scr[...], w_ref[...])
    bd = bd_ref[...]
    for c in range(y.shape[1] // LANES):
        yc = y[:, c * LANES:(c + 1) * LANES]
        hi, lo = _split_bf16(yc * yc)
        ssum = _dot(hi, bd) + _dot(lo, bd)
        r = lax.rsqrt(ssum * (1.0 / HEAD_DIM) + NORM_EPS)
        sl = slice(c * LANES, (c + 1) * LANES)
        scale = jnp.where(flag_ref[:, sl] > 0.5, r, 1.0) * gain_ref[:, sl]
        o_ref[:, sl] = (yc * scale).astype(BF16)


def _inproj(x2, g, w_bf, gain_col, flag_col, tm=512, tn=1536):
    T, D = x2.shape
    N = w_bf.shape[1]
    blk = np.kron(np.eye(LANES // HEAD_DIM), np.ones((HEAD_DIM, HEAD_DIM))).astype(np.float32)
    bd = jnp.asarray(blk, dtype=BF16)
    return pl.pallas_call(
        _inproj_kernel,
        out_shape=jax.ShapeDtypeStruct((T, N), BF16),
        grid=(T // tm, N // tn),
        in_specs=[
            pl.BlockSpec((tm, D), lambda i, j: (i, 0)),
            pl.BlockSpec((1, D), lambda i, j: (0, 0)),
            pl.BlockSpec((D, tn), lambda i, j: (0, j)),
            pl.BlockSpec((1, tn), lambda i, j: (0, j)),
            pl.BlockSpec((1, tn), lambda i, j: (0, j)),
            pl.BlockSpec((LANES, LANES), lambda i, j: (0, 0)),
        ],
        out_specs=pl.BlockSpec((tm, tn), lambda i, j: (i, j)),
        scratch_shapes=[pltpu.VMEM((tm, D), BF16)],
        compiler_params=_params(("arbitrary", "arbitrary")),
        name="inproj",
    )(x2, g, w_bf, gain_col, flag_col, bd)


def _kmean_kernel(k_ref, o_ref):
    nblk = o_ref.shape[1]
    for j in range(nblk):
        kj = k_ref[0, j * MOBA_BLOCK:(j + 1) * MOBA_BLOCK, :].astype(F32)
        o_ref[0, j:j + 1, :] = jnp.sum(kj, axis=0, keepdims=True) * (1.0 / MOBA_BLOCK)


def _kmean(proj3):
    B, S, _ = proj3.shape
    nblk = S // MOBA_BLOCK
    W = MOBA_HEADS * HEAD_DIM
    return pl.pallas_call(
        _kmean_kernel,
        out_shape=jax.ShapeDtypeStruct((B, nblk, W), F32),
        grid=(B, W // 512),
        in_specs=[pl.BlockSpec((1, S, 512), lambda b, w: (b, 0, _KB * LANES // 512 + w))],
        out_specs=pl.BlockSpec((1, nblk, 512), lambda b, w: (b, 0, w)),
        compiler_params=_params(("arbitrary", "arbitrary")),
        name="kmean",
    )(proj3)


def _select_kernel(qlo_ref, qhi_ref, kmh_ref, kml_ref, o_ref):
    c = pl.program_id(1)
    half_k = qlo_ref.shape[2]
    qlo = qlo_ref[0]
    qhi = qhi_ref[0]
    g = (_dot(qlo, kmh_ref[0, :half_k, :]) + _dot(qhi, kmh_ref[0, half_k:, :])
         + _dot(qlo, kml_ref[0, :half_k, :]) + _dot(qhi, kml_ref[0, half_k:, :]))
    nb = _GATE_SLOTS
    lane = lax.broadcasted_iota(I32, (g.shape[0], LANES), 1)
    j = lane & (nb - 1)
    for half in range(g.shape[1] // LANES):
        gh = g[:, half * LANES:(half + 1) * LANES]
        cnt = jnp.zeros(gh.shape, F32)
        for d in range(1, nb):
            a = pltpu.roll(gh, LANES - d, 1)
            b = pltpu.roll(gh, nb - d, 1)
            wrap = (j + d) >= nb
            partner = jnp.where(wrap, b, a)
            jp = jnp.where(wrap, j + (d - nb), j + d)
            ahead = jnp.where(jp < j, jnp.where(partner >= gh, 1.0, 0.0), jnp.where(partner > gh, 1.0, 0.0))
            cnt = cnt + jnp.where(jp < c, ahead, 0.0)
        sel = jnp.where(j < c, cnt, 1e9) < (MOBA_TOPK - 0.5)
        o_ref[0, :, half * LANES:(half + 1) * LANES] = jnp.where(sel, 0.0, NEG).astype(BF16)


def _select(proj3, kmbd_hi, kmbd_lo):
    B, S, _ = proj3.shape
    nq = S // MOBA_BLOCK
    KW = MOBA_HEADS * HEAD_DIM
    NW = kmbd_hi.shape[2]
    qb0 = _QB * LANES // 512
    return pl.pallas_call(
        _select_kernel,
        out_shape=jax.ShapeDtypeStruct((B, S, NW), BF16),
        grid=(B, nq),
        in_specs=[
            pl.BlockSpec((1, MOBA_BLOCK, 512), lambda b, c: (b, c, qb0)),
            pl.BlockSpec((1, MOBA_BLOCK, 512), lambda b, c: (b, c, qb0 + 1)),
            pl.BlockSpec((1, KW, NW), lambda b, c: (b, 0, 0)),
            pl.BlockSpec((1, KW, NW), lambda b, c: (b, 0, 0)),
        ],
        out_specs=pl.BlockSpec((1, MOBA_BLOCK, NW), lambda b, c: (b, c, 0)),
        compiler_params=_params(("arbitrary", "arbitrary")),
        name="moba_select",
    )(proj3, proj3, kmbd_hi, kmbd_lo)


def _swa_kernel(q_ref, kc_ref, kp_ref, vc_ref, vp_ref, bias_ref, sink_ref, o_ref, k_scr, v_scr):
    W = SWA_WINDOW
    tq = q_ref.shape[1]
    first = pl.program_id(2) == 0
    k_scr[0:W, :] = kp_ref[0]
    k_scr[W:, :] = kc_ref[0]
    v_scr[0:W, :] = vp_ref[0]
    v_scr[W:, :] = vc_ref[0]
    lane = lax.broadcasted_iota(I32, (W, LANES), 1)
    keep_lo, keep_hi = _head_lane_masks(W)
    col = lax.broadcasted_iota(I32, (1, 2 * W), 1)
    nopast = jnp.where(jnp.logical_and(first, col < W), NEG, 0.0)
    for sb in range(tq // W):
        k2 = k_scr[sb * W:sb * W + 2 * W, :]
        v2 = v_scr[sb * W:sb * W + 2 * W, :]
        for c in range(q_ref.shape[2] // LANES):
            qc = q_ref[0, sb * W:(sb + 1) * W, c * LANES:(c + 1) * LANES]
            qab = jnp.concatenate([qc * keep_lo, qc * keep_hi], axis=0)
            s = _dot_nt(qab, k2) + bias_ref[0, c]
            if sb == 0:
                s = s + nopast
            sink = sink_ref[0, c]
            m = jnp.maximum(jnp.max(s, axis=-1, keepdims=True), sink)
            p = jnp.exp(s - m)
            l = jnp.sum(p, axis=-1, keepdims=True) + jnp.exp(sink - m)
            o = _dot(p.astype(BF16), v2) / l
            o_ref[0, sb * W:(sb + 1) * W, c * LANES:(c + 1) * LANES] = jnp.where(
                lane < HEAD_DIM, o[:W], o[W:]).astype(BF16)


def _swa(proj3, bias_swa, sink_swa, tq=512):
    B, S, _ = proj3.shape
    W = SWA_WINDOW
    r = tq // W
    npair = SWA_KV_HEADS // 2
    qw = SWA_Q_HEADS * HEAD_DIM // npair
    nc = qw // LANES
    return pl.pallas_call(
        _swa_kernel,
        out_shape=jax.ShapeDtypeStruct((B, S, SWA_Q_HEADS * HEAD_DIM), BF16),
        grid=(B, npair, S // tq),
        in_specs=[
            pl.BlockSpec((1, tq, qw), lambda b, p, i: (b, i, p)),
            pl.BlockSpec((1, tq, LANES), lambda b, p, i: (b, i, _KA + p)),
            pl.BlockSpec((1, W, LANES), lambda b, p, i: (b, jnp.maximum(i * r - 1, 0), _KA + p)),
            pl.BlockSpec((1, tq, LANES), lambda b, p, i: (b, i, _VA + p)),
            pl.BlockSpec((1, W, LANES), lambda b, p, i: (b, jnp.maximum(i * r - 1, 0), _VA + p)),
            pl.BlockSpec((1, nc, 2 * W, 2 * W), lambda b, p, i: (p, 0, 0, 0)),
            pl.BlockSpec((1, nc, 2 * W, 1), lambda b, p, i: (p, 0, 0, 0)),
        ],
        out_specs=pl.BlockSpec((1, tq, qw), lambda b, p, i: (b, i, p)),
        scratch_shapes=[pltpu.VMEM((tq + W, LANES), BF16), pltpu.VMEM((tq + W, LANES), BF16)],
        compiler_params=_params(("arbitrary", "arbitrary", "arbitrary")),
        name="swa_attn",
    )(proj3, proj3, proj3, proj3, proj3, bias_swa, sink_swa)


_MOBA_HG = _HEADS_PER_MASK_BLOCK


_VROWS = 80


def _moba_kernel(q_ref, k_ref, vt_ref, m_ref, bo_ref, ba_ref, far_ref, o_ref,
                 qaug_scr, sa_scr, sb_scr, m_scr, acc_scr):
    L = MOBA_BLOCK
    c = pl.program_id(2)
    lane = lax.broadcasted_iota(I32, (L, LANES), 1)
    keep = _head_lane_masks(L)
    mblk = m_ref[0]

    def rows(j):
        return pl.ds(pl.multiple_of(j * L, L), L)

    def pair(hl):
        return slice((hl // 2) * LANES, (hl // 2 + 1) * LANES)

    for hl in range(_MOBA_HG):
        qaug_scr[hl, :, 0:LANES] = q_ref[0, :, pair(hl)] * keep[hl % 2]
        qaug_scr[hl, :, LANES:2 * LANES] = mblk
    m_scr[...] = jnp.full(m_scr.shape, NEG, F32)
    acc_scr[...] = jnp.zeros(acc_scr.shape, F32)

    def scores(hl, j, s_buf, masked=True):
        kj = k_ref[0, rows(j), pair(hl)]
        if masked:
            onehot = jnp.where(lane == hl * _GATE_SLOTS + j, 1.0, 0.0).astype(BF16)
            s_buf[hl] = _dot_nt(jnp.concatenate([kj, onehot], axis=1), qaug_scr[hl])
        else:
            s_buf[hl] = _dot_nt(kj, qaug_scr[hl, :, 0:LANES])

    def softmax_pv(hl, j, s_buf, kind):
        s = s_buf[hl]
        cb = 0.0
        if kind == "own":
            s = s + bo_ref[hl]
        elif kind == "adj":
            s = s + ba_ref[hl]
        else:
            cb = far_ref[hl, :, 0:1]
        m_old = m_scr[hl]
        m_new = jnp.maximum(m_old, jnp.max(s, axis=0, keepdims=True) + cb)
        pr = jnp.exp(s - (m_new - cb)).astype(BF16)
        m_scr[hl] = m_new
        vt = vt_ref[0, hl * _VROWS:(hl + 1) * _VROWS, rows(j)]
        acc_scr[hl] = acc_scr[hl] * jnp.exp(m_old - m_new) + _dot(vt, pr)

    def stage(qk, sm):
        for hl in range(_MOBA_HG):
            if qk is not None:
                scores(hl, *qk)
            if sm is not None:
                softmax_pv(hl, *sm)

    n_far = jnp.maximum(c - 1, 0)
    n_pair = (n_far + 1) // 2

    @pl.when(c >= 1)
    def _():
        stage((0, sa_scr), None)

    def far_body(tt, carry):
        t0 = 2 * tt
        j1 = jnp.where(t0 + 1 < n_far, t0 + 1, c)
        stage((j1, sb_scr), (t0, sa_scr, "far"))
        stage((jnp.minimum(t0 + 2, c - 1), sa_scr), (j1, sb_scr, "far"))
        return carry

    lax.fori_loop(0, n_pair, far_body, 0)

    @pl.when(c >= 1)
    def _():
        stage((c, sb_scr, False), (c - 1, sa_scr, "adj"))

    @pl.when(c == 0)
    def _():
        stage((c, sb_scr, False), None)

    stage(None, (c, sb_scr, "own"))

    for pr_ in range(_MOBA_HG // 2):
        a0 = acc_scr[2 * pr_]
        a1 = acc_scr[2 * pr_ + 1]
        ot = jnp.concatenate([a0[:HEAD_DIM] / a0[HEAD_DIM:HEAD_DIM + 1],
                              a1[:HEAD_DIM] / a1[HEAD_DIM:HEAD_DIM + 1]], axis=0)
        o_ref[0, :, pr_ * LANES:(pr_ + 1) * LANES] = ot.T.astype(BF16)


def _moba(proj3, vbt, mask, bias_own_t, bias_adj_t, far):
    B, S, _ = proj3.shape
    L = MOBA_BLOCK
    hg = _MOBA_HG
    gw = hg * HEAD_DIM
    ng = MOBA_HEADS // hg
    return pl.pallas_call(
        _moba_kernel,
        out_shape=jax.ShapeDtypeStruct((B, S, MOBA_HEADS * HEAD_DIM), BF16),
        grid=(B, ng, S // L),
        in_specs=[
            pl.BlockSpec((1, L, gw), lambda b, g, c: (b, c, _QB * LANES // gw + g)),
            pl.BlockSpec((1, S, gw), lambda b, g, c: (b, 0, _KB * LANES // gw + g)),
            pl.BlockSpec((1, hg * _VROWS, S), lambda b, g, c: (b, g, 0)),
            pl.BlockSpec((1, L, LANES), lambda b, g, c: (b, c, g)),
            pl.BlockSpec((hg, L, L), lambda b, g, c: (g, 0, 0)),
            pl.BlockSpec((hg, L, L), lambda b, g, c: (g, 0, 0)),
            pl.BlockSpec((hg, 1, LANES), lambda b, g, c: (g, 0, 0)),
        ],
        out_specs=pl.BlockSpec((1, L, gw), lambda b, g, c: (b, c, g)),
        scratch_shapes=[pltpu.VMEM((hg, L, 2 * LANES), BF16), pltpu.VMEM((hg, L, L), F32),
                        pltpu.VMEM((hg, L, L), F32), pltpu.VMEM((hg, 1, L), F32),
                        pltpu.VMEM((hg, _VROWS, L), F32)],
        compiler_params=_params(("arbitrary", "arbitrary", "arbitrary")),
        name="moba_attn",
    )(proj3, proj3, vbt, mask, bias_own_t, bias_adj_t, far)


def _outproj_kernel(x_ref, oa_ref, ob_ref, wa_ref, wb_ref, g_ref, wrh_ref, wrl_ref, br_ref,
                    x1_ref, h2_ref, idx_ref, gate_ref, rank_ref, cnt_ref, run_scr):
    i = pl.program_id(0)

    @pl.when(i == 0)
    def _():
        run_scr[...] = jnp.zeros_like(run_scr)

    x1 = x_ref[...] + _dot(oa_ref[...], wa_ref[...]) + _dot(ob_ref[...], wb_ref[...])
    x1_ref[...] = x1
    ms = jnp.mean(x1 * x1, axis=-1, keepdims=True)
    h2 = x1 * lax.rsqrt(ms + NORM_EPS) * g_ref[...]
    h2_ref[...] = h2
    hh, hl = _split_bf16(h2)
    logits = (_dot(hh, wrh_ref[...]) + _dot(hl, wrh_ref[...]) + _dot(hh, wrl_ref[...])) + br_ref[...]

    tm = logits.shape[0]
    lane = lax.broadcasted_iota(I32, (tm, LANES), 1)
    vals, idxs = [], []
    for _k in range(TOP_K):
        mx = jnp.max(logits, axis=-1, keepdims=True)
        ix = jnp.min(jnp.where(logits == mx, lane, LANES), axis=-1, keepdims=True)
        vals.append(mx)
        idxs.append(ix)
        logits = jnp.where(lane == ix, -3e38, logits)
    es = [jnp.exp(v - vals[0]) for v in vals]
    den = es[0] + es[1] + es[2] + es[3]

    onehots = [lane == ix for ix in idxs]
    ohsum = jnp.zeros((tm, LANES), F32)
    for oh in onehots:
        ohsum = ohsum + jnp.where(oh, 1.0, 0.0)
    r_i = lax.broadcasted_iota(I32, (tm, tm), 0)
    c_i = lax.broadcasted_iota(I32, (tm, tm), 1)
    lower = jnp.where(r_i > c_i, 1.0, 0.0).astype(BF16)
    base = run_scr[0:1, :] + _dot(lower, ohsum.astype(BF16))

    idx_out = jnp.zeros((tm, LANES), I32)
    gate_out = jnp.zeros((tm, LANES), F32)
    rank_out = jnp.zeros((tm, LANES), F32)
    for k in range(TOP_K):
        rk = jnp.sum(jnp.where(onehots[k], base, 0.0), axis=-1, keepdims=True)
        idx_out = jnp.where(lane == k, idxs[k], idx_out)
        gate_out = jnp.where(lane == k, es[k] / den, gate_out)
        rank_out = jnp.where(lane == k, rk, rank_out)
    idx_ref[...] = idx_out
    gate_ref[...] = gate_out
    rank_ref[...] = rank_out.astype(I32)
    run = run_scr[0:1, :] + jnp.sum(ohsum, axis=0, keepdims=True)
    run_scr[...] = jnp.broadcast_to(run, run_scr.shape)
    cnt_ref[...] = jnp.broadcast_to(run, cnt_ref.shape).astype(I32)


def _outproj(x2, oa, ob, wa, wb, g, wr_hi, wr_lo, br, tm=256):
    T, D = x2.shape
    Ka = oa.shape[1]
    Kb = ob.shape[1]
    row = lambda i: (i, 0)
    fix = lambda i: (0, 0)
    return pl.pallas_call(
        _outproj_kernel,
        out_shape=(
            jax.ShapeDtypeStruct((T, D), F32),
            jax.ShapeDtypeStruct((T, D), F32),
            jax.ShapeDtypeStruct((T, LANES), I32),
            jax.ShapeDtypeStruct((T, LANES), F32),
            jax.ShapeDtypeStruct((T, LANES), I32),
            jax.ShapeDtypeStruct((8, LANES), I32),
        ),
        grid=(T // tm,),
        in_specs=[
            pl.BlockSpec((tm, D), row),
            pl.BlockSpec((tm, Ka), row),
            pl.BlockSpec((tm, Kb), row),
            pl.BlockSpec((Ka, D), fix),
            pl.BlockSpec((Kb, D), fix),
            pl.BlockSpec((1, D), fix),
            pl.BlockSpec((D, LANES), fix),
            pl.BlockSpec((D, LANES), fix),
            pl.BlockSpec((1, LANES), fix),
        ],
        out_specs=(
            pl.BlockSpec((tm, D), row),
            pl.BlockSpec((tm, D), row),
            pl.BlockSpec((tm, LANES), row),
            pl.BlockSpec((tm, LANES), row),
            pl.BlockSpec((tm, LANES), row),
            pl.BlockSpec((8, LANES), fix),
        ),
        scratch_shapes=[pltpu.VMEM((8, LANES), F32)],
        compiler_params=_params(("arbitrary",)),
        name="outproj_router",
    )(x2, oa, ob, wa, wb, g, wr_hi, wr_lo, br)


_PERM_W = 256


def _prep_gu_kernel(w_ref, perm_ref, o_ref):
    pm = perm_ref[...]
    F = o_ref.shape[2] // 2
    hw = _PERM_W // 2
    for g in range(w_ref.shape[2] // _PERM_W):
        wt = w_ref[0, :, g * _PERM_W:(g + 1) * _PERM_W].astype(BF16)
        d = _dot(wt, pm).astype(BF16)
        o_ref[0, :, g * hw:(g + 1) * hw] = d[:, :hw]
        o_ref[0, :, F + g * hw:F + (g + 1) * hw] = d[:, hw:]


def _prep_gu(w_gu, tk=512):
    E, K, N2 = w_gu.shape
    idx = np.arange(_PERM_W)
    src = np.where(idx < _PERM_W // 2, 2 * idx, 2 * (idx - _PERM_W // 2) + 1)
    pm = np.zeros((_PERM_W, _PERM_W), np.float32)
    pm[src, idx] = 1.0
    return pl.pallas_call(
        _prep_gu_kernel,
        out_shape=jax.ShapeDtypeStruct((E, K, N2), BF16),
        grid=(E, K // tk),
        in_specs=[pl.BlockSpec((1, tk, N2), lambda e, k: (e, k, 0)),
                  pl.BlockSpec((_PERM_W, _PERM_W), lambda e, k: (0, 0))],
        out_specs=pl.BlockSpec((1, tk, N2), lambda e, k: (e, k, 0)),
        compiler_params=_params(("arbitrary", "arbitrary")),
        name="moe_prep_gate_up",
    )(w_gu, jnp.asarray(pm, dtype=BF16))


def _issue_row_gather(idx_ref, src_hbm, dst, sem):
    def body(r, carry):
        pltpu.make_async_copy(src_hbm.at[pl.ds(idx_ref[0, 0, r], 1)], dst.at[pl.ds(r, 1)], sem).start()
        return carry

    lax.fori_loop(0, dst.shape[0], body, 0)


def _moe_gu_kernel(be_ref, tokc_ref, tokn_ref, h_hbm, w_ref, bg_ref, bl_ref, o_ref, xbuf, sem):
    i = pl.program_id(0)
    slot = i % 2
    bm = xbuf.shape[1]

    @pl.when(i == 0)
    def _():
        _issue_row_gather(tokc_ref, h_hbm, xbuf.at[0], sem.at[0])

    @pl.when(i + 1 < pl.num_programs(0))
    def _():
        _issue_row_gather(tokn_ref, h_hbm, xbuf.at[1 - slot], sem.at[1 - slot])

    pltpu.make_async_copy(h_hbm.at[pl.ds(0, bm)], xbuf.at[slot], sem.at[slot]).wait()
    x = xbuf[slot].astype(BF16)
    F = o_ref.shape[1]
    nh = 2
    fh = F // nh
    for h in range(nh):
        xg = _dot(x, w_ref[0, :, h * fh:(h + 1) * fh]) + bg_ref[0, :, h * fh:(h + 1) * fh]
        xl = _dot(x, w_ref[0, :, F + h * fh:F + (h + 1) * fh]) + bl_ref[0, :, h * fh:(h + 1) * fh]
        xg = jnp.minimum(xg, SWIGLU_LIMIT)
        xl = jnp.clip(xl, -SWIGLU_LIMIT, SWIGLU_LIMIT)
        act = xg * jax.nn.sigmoid(SWIGLU_ALPHA * xg) * (xl + 1.0)
        o_ref[:, h * fh:(h + 1) * fh] = act.astype(BF16)


def _moe_gu(block_expert, buf_tok, h2, w_gu_bf, b_gate, b_lin, bm=MOE_BLOCK):
    P = buf_tok.shape[0]
    T, D = h2.shape
    E, _, N2 = w_gu_bf.shape
    F = N2 // 2
    nb = P // bm
    tok3 = buf_tok.reshape(nb, 1, bm)
    return pl.pallas_call(
        _moe_gu_kernel,
        out_shape=jax.ShapeDtypeStruct((P, F), BF16),
        grid_spec=pltpu.PrefetchScalarGridSpec(
            num_scalar_prefetch=1,
            grid=(nb,),
            in_specs=[
                pl.BlockSpec((1, 1, bm), lambda i, be: (i, 0, 0), memory_space=pltpu.SMEM),
                pl.BlockSpec((1, 1, bm), lambda i, be: (jnp.minimum(i + 1, nb - 1), 0, 0),
                             memory_space=pltpu.SMEM),
                pl.BlockSpec(memory_space=pl.ANY),
                pl.BlockSpec((1, D, N2), lambda i, be: (be[i], 0, 0)),
                pl.BlockSpec((1, 1, F), lambda i, be: (be[i], 0, 0)),
                pl.BlockSpec((1, 1, F), lambda i, be: (be[i], 0, 0)),
            ],
            out_specs=pl.BlockSpec((bm, F), lambda i, be: (i, 0)),
            scratch_shapes=[pltpu.VMEM((2, bm, D), F32), pltpu.SemaphoreType.DMA((2,))],
        ),
        compiler_params=_params(("arbitrary",)),
        name="moe_gather_gate_up",
    )(block_expert, tok3, tok3, h2, w_gu_bf, b_gate, b_lin)


def _moe_down_kernel(be_ref, nv_ref, inv_ref, h_ref, w_ref, b_ref, yt_hbm, ybuf, sem, w_scr):
    i = pl.program_id(0)
    n = pl.num_programs(0)
    slot = i % 2
    bm = ybuf.shape[1]

    def drain(step, s):
        nv = nv_ref[step]

        @pl.when(nv == bm)
        def _():
            pltpu.make_async_copy(ybuf.at[s], yt_hbm.at[pl.ds(0, bm)], sem.at[s]).wait()

        @pl.when(nv < bm)
        def _():
            def body(r, carry):
                pltpu.make_async_copy(ybuf.at[s, pl.ds(0, 1)], yt_hbm.at[pl.ds(0, 1)], sem.at[s]).wait()
                return carry

            lax.fori_loop(0, nv, body, 0)

    @pl.when(i >= 2)
    def _():
        drain(i - 2, slot)

    prev = be_ref[jnp.maximum(i - 1, 0)]

    @pl.when(jnp.logical_or(i == 0, be_ref[i] != prev))
    def _():
        w_scr[...] = w_ref[0].astype(BF16)

    ybuf[slot] = _dot(h_ref[...], w_scr[...]) + b_ref[0]

    def issue(r, carry):
        pltpu.make_async_copy(ybuf.at[slot, pl.ds(r, 1)], yt_hbm.at[pl.ds(inv_ref[0, 0, r], 1)],
                              sem.at[slot]).start()
        return carry

    lax.fori_loop(0, nv_ref[i], issue, 0)

    @pl.when(i == n - 1)
    def _():
        @pl.when(n >= 2)
        def _():
            drain(i - 1, 1 - slot)
        drain(i, slot)


def _moe_down(block_expert, n_valid, inv, hact, w_down, b_down, n_rows, bm=MOE_BLOCK):
    P, F = hact.shape
    E, _, D = w_down.shape
    nb = P // bm
    return pl.pallas_call(
        _moe_down_kernel,
        out_shape=jax.ShapeDtypeStruct((n_rows, D), F32),
        grid_spec=pltpu.PrefetchScalarGridSpec(
            num_scalar_prefetch=2,
            grid=(nb,),
            in_specs=[
                pl.BlockSpec((1, 1, bm), lambda i, be, nv: (i, 0, 0), memory_space=pltpu.SMEM),
                pl.BlockSpec((bm, F), lambda i, be, nv: (i, 0)),
                pl.BlockSpec((1, F, D), lambda i, be, nv: (be[i], 0, 0)),
                pl.BlockSpec((1, 1, D), lambda i, be, nv: (be[i], 0, 0)),
            ],
            out_specs=pl.BlockSpec(memory_space=pl.ANY),
            scratch_shapes=[pltpu.VMEM((2, bm, D), F32), pltpu.SemaphoreType.DMA((2,)),
                            pltpu.VMEM((F, D), BF16)],
        ),
        compiler_params=_params(("arbitrary",)),
        name="moe_down_scatter",
    )(block_expert, n_valid, inv.reshape(nb, 1, bm), hact, w_down, b_down)


def _combine_kernel(x1_ref, gate_ref, y_ref, o_ref):
    tb = x1_ref.shape[0]
    acc = x1_ref[...]
    gates = gate_ref[...]
    for k in range(TOP_K):
        acc = acc + y_ref[k * tb:(k + 1) * tb, :] * gates[:, k:k + 1]
    o_ref[...] = acc


def _combine(x1, gates, yt, tb):
    T, D = x1.shape
    return pl.pallas_call(
        _combine_kernel,
        out_shape=jax.ShapeDtypeStruct((T, D), F32),
        grid=(T // tb,),
        in_specs=[
            pl.BlockSpec((tb, D), lambda i: (i, 0)),
            pl.BlockSpec((tb, LANES), lambda i: (i, 0)),
            pl.BlockSpec((TOP_K * tb, D), lambda i: (i, 0)),
        ],
        out_specs=pl.BlockSpec((tb, D), lambda i: (i, 0)),
        compiler_params=_params(("arbitrary",)),
        name="moe_combine",
    )(x1, gates, yt)


def _t5_bucket(dist):
    max_exact = T5_BUCKETS // 2
    d = jnp.maximum(dist, 0)
    df = jnp.maximum(d, 1).astype(F32)
    large = max_exact + (jnp.log(df / max_exact) / math.log(T5_MAX_DISTANCE / max_exact)
                         * (T5_BUCKETS - max_exact)).astype(I32)
    large = jnp.minimum(large, T5_BUCKETS - 1)
    return jnp.where(d < max_exact, d, large)


def _toeplitz_bias(rel_t, R, C, offset, lo, hi):
    H = rel_t.shape[0]
    d = np.arange(R + C - 1) - (C - 1) + offset
    w = jnp.where(jnp.asarray((d >= lo) & (d < hi))[None], rel_t[:, _t5_bucket(jnp.asarray(d, I32))], NEG)
    m = R + C
    w_ext = jnp.pad(w[:, ::-1], ((0, 0), (0, 1)))
    y = jnp.tile(w_ext, (1, R))[:, :R * (m - 1)].reshape(H, R, m - 1)
    return y[:, :, R - 1:R - 1 + C].astype(F32)


def _pair_major(a, n_pair, n_half, n_c):
    sh = a.shape
    a = a.reshape((n_pair, n_half, n_c) + sh[1:])
    a = jnp.swapaxes(a, 1, 2)
    return a.reshape(sh)


def kernel(x, attn_norm_g, w_in, swa_q_gain, swa_k_gain, swa_sinks, moba_q_gain, moba_k_gain, rel_bias,
           w_out, ffn_norm_g, w_router, b_router, w_gate_up, b_gate_up, w_down, b_down):
    B, S, D = x.shape
    T = B * S
    assert w_in.shape[0] == 1, "single-layer kernel"
    qa_w = SWA_Q_HEADS * HEAD_DIM
    kv_w = SWA_KV_HEADS * HEAD_DIM
    mb_w = MOBA_HEADS * HEAD_DIM
    G = SWA_Q_HEADS // SWA_KV_HEADS
    npair = SWA_KV_HEADS // 2

    w0 = w_in[0]
    wq = w0[:, :qa_w].reshape(D, SWA_Q_HEADS, HEAD_DIM)
    wq = jnp.swapaxes(wq.reshape(D, npair, 2, G, HEAD_DIM), 2, 3).reshape(D, qa_w)
    w_in_bf = jnp.concatenate([wq, w0[:, qa_w:]], axis=1).astype(BF16)
    ones = jnp.ones((HEAD_DIM,), F32)
    tile = lambda v, n: jnp.tile(v.astype(F32), n)
    gain_col = jnp.concatenate([
        tile(swa_q_gain[0] * ATTN_SCALE, SWA_Q_HEADS), tile(swa_k_gain[0], SWA_KV_HEADS),
        tile(ones, SWA_KV_HEADS), tile(moba_q_gain[0] * ATTN_SCALE, MOBA_HEADS),
        tile(moba_k_gain[0], MOBA_HEADS), tile(ones, MOBA_HEADS)])[None, :]
    flag_np = np.concatenate([np.ones(qa_w + kv_w), np.zeros(kv_w), np.ones(2 * mb_w), np.zeros(mb_w)])
    flag_col = jnp.asarray(flag_np[None, :], F32)

    wo = w_out[0]
    wa = wo[:qa_w].reshape(npair, 2, G, HEAD_DIM, D)
    wa = jnp.swapaxes(wa, 1, 2).reshape(qa_w, D).astype(BF16)
    wb = wo[qa_w:].astype(BF16)

    rel_a = rel_bias[:, :SWA_Q_HEADS].T.astype(F32)
    rel_b = rel_bias[:, SWA_Q_HEADS:].T.astype(F32)
    W = SWA_WINDOW
    bias_a = _toeplitz_bias(rel_a, W, 2 * W, W, 0, W)
    bias_swa = _pair_major(bias_a, npair, 2, G).reshape(npair, G, 2 * W, 2 * W)
    sink_col = jnp.broadcast_to(swa_sinks[0].astype(F32)[:, None, None], (SWA_Q_HEADS, W, 1))
    sink_swa = _pair_major(sink_col, npair, 2, G).reshape(npair, G, 2 * W, 1)
    L = MOBA_BLOCK
    bias_own_t = jnp.swapaxes(_toeplitz_bias(rel_b, L, L, 0, 0, 2 * L), 1, 2)
    bias_adj_t = jnp.swapaxes(_toeplitz_bias(rel_b, L, L, L, 0, 2 * L), 1, 2)
    far = jnp.broadcast_to(rel_b[:, T5_BUCKETS - 1][:, None, None], (MOBA_HEADS, 1, LANES))

    x2 = x.reshape(T, D)
    proj = _inproj(x2, attn_norm_g[0][None, :].astype(F32), w_in_bf, gain_col, flag_col)
    proj3 = proj.reshape(B, S, proj.shape[1])
    kmean = _kmean(proj3)
    nblk = S // L
    assert nblk <= _GATE_SLOTS, "selection mask packs at most 16 key blocks per head"
    km = kmean.reshape(B, nblk, MOBA_HEADS, HEAD_DIM).transpose(0, 2, 3, 1)
    km = jnp.pad(km, ((0, 0), (0, 0), (0, 0), (0, _GATE_SLOTS - nblk)))
    eye = jnp.eye(MOBA_HEADS, dtype=F32)
    kmbd = (km[:, :, :, None, :] * eye[None, :, None, :, None]).reshape(B, mb_w, MOBA_HEADS * _GATE_SLOTS)
    kmbd_hi, kmbd_lo = _split_bf16(kmbd)
    mask = _select(proj3, kmbd_hi, kmbd_lo)
    oa = _swa(proj3, bias_swa, sink_swa)
    vbt = jnp.swapaxes(proj3[:, :, _VB * LANES:], 1, 2).reshape(B, MOBA_HEADS, HEAD_DIM, S)
    vbt = jnp.concatenate([vbt, jnp.ones((B, MOBA_HEADS, 1, S), BF16),
                           jnp.zeros((B, MOBA_HEADS, _VROWS - HEAD_DIM - 1, S), BF16)],
                          axis=2).reshape(B, MOBA_HEADS * _VROWS, S)
    ob = _moba(proj3, vbt, mask, bias_own_t, bias_adj_t, far)

    wr = jnp.pad(w_router[0].astype(F32), ((0, 0), (0, LANES - N_EXPERTS)))
    wr_hi, wr_lo = _split_bf16(wr)
    br = jnp.pad(b_router[0].astype(F32), (0, LANES - N_EXPERTS), constant_values=NEG)[None, :]
    x1, h2, idx_o, gate_o, rank_o, cnt_o = _outproj(
        x2, oa.reshape(T, qa_w), ob.reshape(T, mb_w), wa, wb, ffn_norm_g[0][None, :].astype(F32),
        wr_hi, wr_lo, br)

    A = T * TOP_K
    counts = cnt_o[0, :N_EXPERTS]
    padded = (counts + MOE_BLOCK - 1) // MOE_BLOCK * MOE_BLOCK
    pad_ends = jnp.cumsum(padded)
    pad_starts = pad_ends - padded
    n_blocks = -(-(A + N_EXPERTS * (MOE_BLOCK - 1)) // MOE_BLOCK)
    P = n_blocks * MOE_BLOCK
    top_idx = idx_o[:, :TOP_K]
    is_e = top_idx[:, :, None] == jnp.arange(N_EXPERTS, dtype=I32)
    dest = jnp.sum(jnp.where(is_e, pad_starts, 0), axis=-1) + rank_o[:, :TOP_K]
    tb = _COMBINE_TB
    t_idx = jnp.arange(T, dtype=I32)[:, None]
    out_row = (t_idx // tb) * (TOP_K * tb) + jnp.arange(TOP_K, dtype=I32)[None, :] * tb + t_idx % tb
    inv = jnp.full((P,), -1, I32).at[dest.reshape(A)].set(out_row.reshape(A))
    inv_c = jnp.maximum(inv, 0)
    buf_tok = jnp.where(inv >= 0, (inv_c // (TOP_K * tb)) * tb + inv_c % tb, 0)
    blk_start = jnp.arange(n_blocks, dtype=I32)[:, None] * MOE_BLOCK
    block_expert = jnp.minimum(jnp.sum((pad_ends[None, :] <= blk_start).astype(I32), axis=1),
                               N_EXPERTS - 1).astype(I32)
    is_be = block_expert[:, None] == jnp.arange(N_EXPERTS, dtype=I32)[None, :]
    used = blk_start[:, 0] - jnp.sum(jnp.where(is_be, pad_starts[None, :], 0), axis=1)
    n_valid = jnp.clip(jnp.sum(jnp.where(is_be, counts[None, :], 0), axis=1) - used, 0, MOE_BLOCK).astype(I32)

    bgu = b_gate_up[0].astype(F32).reshape(N_EXPERTS, -1, 2)
    b_gate = bgu[:, :, 0][:, None, :]
    b_lin = bgu[:, :, 1][:, None, :]
    w_gu_bf = _prep_gu(w_gate_up[0])
    hact = _moe_gu(block_expert, buf_tok, h2, w_gu_bf, b_gate, b_lin)
    yt = _moe_down(block_expert, n_valid, inv_c, hact, w_down[0], b_down[0].astype(F32)[:, None, :], A)
    out = _combine(x1, gate_o, yt, tb)
    return out.reshape(B, S, D)
```

```python
import functools
import math

import numpy as np
import jax
import jax.numpy as jnp
from jax import lax
from jax.experimental import pallas as pl
from jax.experimental.pallas import tpu as pltpu

F32 = jnp.float32
BF16 = jnp.bfloat16
I32 = jnp.int32

HEAD_DIM = 64
SWA_Q_HEADS = 16
SWA_KV_HEADS = 4
SWA_WINDOW = 128
MOBA_HEADS = 16
MOBA_BLOCK = 256
MOBA_TOPK = 3
T5_BUCKETS = 32
T5_MAX_DISTANCE = 128
N_EXPERTS = 32
TOP_K = 4
SWIGLU_LIMIT = 7.0
SWIGLU_ALPHA = 1.702
MOE_BLOCK = 256
NORM_EPS = 1e-5
ATTN_SCALE = HEAD_DIM ** -0.5

LANES = 128
NEG = -1e30
VMEM_LIMIT = 56 * 1024 * 1024
_COMBINE_TB = 128
_GATE_SLOTS = 16
_HEADS_PER_MASK_BLOCK = LANES // _GATE_SLOTS

_QA, _KA, _VA, _QB, _KB, _VB = 0, 8, 10, 12, 20, 28


def _dot(a, b):
    return jnp.dot(a, b, preferred_element_type=F32)


def _dot_nt(a, b):
    return lax.dot_general(a, b, (((1,), (1,)), ((), ())), preferred_element_type=F32)


def _split_bf16(x):
    hi = x.astype(BF16)
    lo = (x - hi.astype(F32)).astype(BF16)
    return hi, lo


def _head_lane_masks(rows):
    lane = lax.broadcasted_iota(I32, (rows, LANES), 1)
    lo = jnp.where(lane < HEAD_DIM, 1.0, 0.0).astype(BF16)
    return lo, (1.0 - lo.astype(F32)).astype(BF16)


def _params(sem):
    return pltpu.CompilerParams(dimension_semantics=sem, vmem_limit_bytes=VMEM_LIMIT)


def _inproj_kernel(x_ref, g_ref, w_ref, gain_ref, flag_ref, bd_ref, o_ref, h_scr):
    @pl.when(pl.program_id(1) == 0)
    def _():
        x = x_ref[...]
        ms = jnp.mean(x * x, axis=-1, keepdims=True)
        h_scr[...] = (x * lax.rsqrt(ms + NORM_EPS) * g_ref[...]).astype(BF16)

    y = _dot(h_scr[...], w_ref[...])
    bd = bd_ref[...]
    for c in range(y.shape[1] // LANES):
        yc = y[:, c * LANES:(c + 1) * LANES]
        hi, lo = _split_bf16(yc * yc)
        ssum = _dot(hi, bd) + _dot(lo, bd)
        r = lax.rsqrt(ssum * (1.0 / HEAD_DIM) + NORM_EPS)
        sl = slice(c * LANES, (c + 1) * LANES)
        scale = jnp.where(flag_ref[:, sl] > 0.5, r, 1.0) * gain_ref[:, sl]
        o_ref[:, sl] = (yc * scale).astype(BF16)


def _inproj(x2, g, w_bf, gain_col, flag_col, tm=512, tn=1536):
    T, D = x2.shape
    N = w_bf.shape[1]
    blk = np.kron(np.eye(LANES // HEAD_DIM), np.ones((HEAD_DIM, HEAD_DIM))).astype(np.float32)
    bd = jnp.asarray(blk, dtype=BF16)
    return pl.pallas_call(
        _inproj_kernel,
        out_shape=jax.ShapeDtypeStruct((T, N), BF16),
        grid=(T // tm, N // tn),
        in_specs=[
            pl.BlockSpec((tm, D), lambda i, j: (i, 0)),
            pl.BlockSpec((1, D), lambda i, j: (0, 0)),
            pl.BlockSpec((D, tn), lambda i, j: (0, j)),
            pl.BlockSpec((1, tn), lambda i, j: (0, j)),
            pl.BlockSpec((1, tn), lambda i, j: (0, j)),
            pl.BlockSpec((LANES, LANES), lambda i, j: (0, 0)),
        ],
        out_specs=pl.BlockSpec((tm, tn), lambda i, j: (i, j)),
        scratch_shapes=[pltpu.VMEM((tm, D), BF16)],
        compiler_params=_params(("arbitrary", "arbitrary")),
        name="inproj",
    )(x2, g, w_bf, gain_col, flag_col, bd)


def _kmean_kernel(k_ref, o_ref):
    nblk = o_ref.shape[1]
    for j in range(nblk):
        kj = k_ref[0, j * MOBA_BLOCK:(j + 1) * MOBA_BLOCK, :].astype(F32)
        o_ref[0, j:j + 1, :] = jnp.sum(kj, axis=0, keepdims=True) * (1.0 / MOBA_BLOCK)


def _kmean(proj3):
    B, S, _ = proj3.shape
    nblk = S // MOBA_BLOCK
    W = MOBA_HEADS * HEAD_DIM
    return pl.pallas_call(
        _kmean_kernel,
        out_shape=jax.ShapeDtypeStruct((B, nblk, W), F32),
        grid=(B, W // 512),
        in_specs=[pl.BlockSpec((1, S, 512), lambda b, w: (b, 0, _KB * LANES // 512 + w))],
        out_specs=pl.BlockSpec((1, nblk, 512), lambda b, w: (b, 0, w)),
        compiler_params=_params(("arbitrary", "arbitrary")),
        name="kmean",
    )(proj3)


def _select_kernel(qlo_ref, qhi_ref, kmh_ref, kml_ref, o_ref):
    c = pl.program_id(1)
    half_k = qlo_ref.shape[2]
    qlo = qlo_ref[0]
    qhi = qhi_ref[0]
    g = (_dot(qlo, kmh_ref[0, :half_k, :]) + _dot(qhi, kmh_ref[0, half_k:, :])
         + _dot(qlo, kml_ref[0, :half_k, :]) + _dot(qhi, kml_ref[0, half_k:, :]))
    nb = _GATE_SLOTS
    lane = lax.broadcasted_iota(I32, (g.shape[0], LANES), 1)
    j = lane & (nb - 1)
    for half in range(g.shape[1] // LANES):
        gh = g[:, half * LANES:(half + 1) * LANES]
        cnt = jnp.zeros(gh.shape, F32)
        for d in range(1, nb):
            a = pltpu.roll(gh, LANES - d, 1)
            b = pltpu.roll(gh, nb - d, 1)
            wrap = (j + d) >= nb
            partner = jnp.where(wrap, b, a)
            jp = jnp.where(wrap, j + (d - nb), j + d)
            ahead = jnp.where(jp < j, jnp.where(partner >= gh, 1.0, 0.0), jnp.where(partner > gh, 1.0, 0.0))
            cnt = cnt + jnp.where(jp < c, ahead, 0.0)
        sel = jnp.where(j < c, cnt, 1e9) < (MOBA_TOPK - 0.5)
        o_ref[0, :, half * LANES:(half + 1) * LANES] = jnp.where(sel, 0.0, NEG).astype(BF16)


def _select(proj3, kmbd_hi, kmbd_lo):
    B, S, _ = proj3.shape
    nq = S // MOBA_BLOCK
    KW = MOBA_HEADS * HEAD_DIM
    NW = kmbd_hi.shape[2]
    qb0 = _QB * LANES // 512
    return pl.pallas_call(
        _select_kernel,
        out_shape=jax.ShapeDtypeStruct((B, S, NW), BF16),
        grid=(B, nq),
        in_specs=[
            pl.BlockSpec((1, MOBA_BLOCK, 512), lambda b, c: (b, c, qb0)),
            pl.BlockSpec((1, MOBA_BLOCK, 512), lambda b, c: (b, c, qb0 + 1)),
            pl.BlockSpec((1, KW, NW), lambda b, c: (b, 0, 0)),
            pl.BlockSpec((1, KW, NW), lambda b, c: (b, 0, 0)),
        ],
        out_specs=pl.BlockSpec((1, MOBA_BLOCK, NW), lambda b, c: (b, c, 0)),
        compiler_params=_params(("arbitrary", "arbitrary")),
        name="moba_select",
    )(proj3, proj3, kmbd_hi, kmbd_lo)


def _swa_kernel(q_ref, kc_ref, kp_ref, vc_ref, vp_ref, bias_ref, sink_ref, o_ref, k_scr, v_scr):
    W = SWA_WINDOW
    tq = q_ref.shape[1]
    first = pl.program_id(2) == 0
    k_scr[0:W, :] = kp_ref[0]
    k_scr[W:, :] = kc_ref[0]
    v_scr[0:W, :] = vp_ref[0]
    v_scr[W:, :] = vc_ref[0]
    lane = lax.broadcasted_iota(I32, (W, LANES), 1)
    keep_lo, keep_hi = _head_lane_masks(W)
    col = lax.broadcasted_iota(I32, (1, 2 * W), 1)
    nopast = jnp.where(jnp.logical_and(first, col < W), NEG, 0.0)
    for sb in range(tq // W):
        k2 = k_scr[sb * W:sb * W + 2 * W, :]
        v2 = v_scr[sb * W:sb * W + 2 * W, :]
        for c in range(q_ref.shape[2] // LANES):
            qc = q_ref[0, sb * W:(sb + 1) * W, c * LANES:(c + 1) * LANES]
            qab = jnp.concatenate([qc * keep_lo, qc * keep_hi], axis=0)
            s = _dot_nt(qab, k2) + bias_ref[0, c]
            if sb == 0:
                s = s + nopast
            sink = sink_ref[0, c]
            m = jnp.maximum(jnp.max(s, axis=-1, keepdims=True), sink)
            p = jnp.exp(s - m)
            l = jnp.sum(p, axis=-1, keepdims=True) + jnp.exp(sink - m)
            o = _dot(p.astype(BF16), v2) / l
            o_ref[0, sb * W:(sb + 1) * W, c * LANES:(c + 1) * LANES] = jnp.where(
                lane < HEAD_DIM, o[:W], o[W:]).astype(BF16)


def _swa(proj3, bias_swa, sink_swa, tq=512):
    B, S, _ = proj3.shape
    W = SWA_WINDOW
    r = tq // W
    npair = SWA_KV_HEADS // 2
    qw = SWA_Q_HEADS * HEAD_DIM // npair
    nc = qw // LANES
    return pl.pallas_call(
        _swa_kernel,
        out_shape=jax.ShapeDtypeStruct((B, S, SWA_Q_HEADS * HEAD_DIM), BF16),
        grid=(B, npair, S // tq),
        in_specs=[
            pl.BlockSpec((1, tq, qw), lambda b, p, i: (b, i, p)),
            pl.BlockSpec((1, tq, LANES), lambda b, p, i: (b, i, _KA + p)),
            pl.BlockSpec((1, W, LANES), lambda b, p, i: (b, jnp.maximum(i * r - 1, 0), _KA + p)),
            pl.BlockSpec((1, tq, LANES), lambda b, p, i: (b, i, _VA + p)),
            pl.BlockSpec((1, W, LANES), lambda b, p, i: (b, jnp.maximum(i * r - 1, 0), _VA + p)),
            pl.BlockSpec((1, nc, 2 * W, 2 * W), lambda b, p, i: (p, 0, 0, 0)),
            pl.BlockSpec((1, nc, 2 * W, 1), lambda b, p, i: (p, 0, 0, 0)),
        ],
        out_specs=pl.BlockSpec((1, tq, qw), lambda b, p, i: (b, i, p)),
        scratch_shapes=[pltpu.VMEM((tq + W, LANES), BF16), pltpu.VMEM((tq + W, LANES), BF16)],
        compiler_params=_params(("arbitrary", "arbitrary", "arbitrary")),
        name="swa_attn",
    )(proj3, proj3, proj3, proj3, proj3, bias_swa, sink_swa)


_MOBA_HG = _HEADS_PER_MASK_BLOCK


_VROWS = 80


def _moba_kernel(q_ref, k_ref, vt_ref, m_ref, bo_ref, ba_ref, far_ref, o_ref,
                 qaug_scr, sa_scr, sb_scr, m_scr, acc_scr):
    L = MOBA_BLOCK
    c = pl.program_id(2)
    lane = lax.broadcasted_iota(I32, (L, LANES), 1)
    keep = _head_lane_masks(L)
    mblk = m_ref[0]

    def rows(j):
        return pl.ds(pl.multiple_of(j * L, L), L)

    def pair(hl):
        return slice((hl // 2) * LANES, (hl // 2 + 1) * LANES)

    for hl in range(_MOBA_HG):
        qaug_scr[hl, :, 0:LANES] = q_ref[0, :, pair(hl)] * keep[hl % 2]
        qaug_scr[hl, :, LANES:2 * LANES] = mblk
    m_scr[...] = jnp.full(m_scr.shape, NEG, F32)
    acc_scr[...] = jnp.zeros(acc_scr.shape, F32)

    def scores(hl, j, s_buf, masked=True):
        kj = k_ref[0, rows(j), pair(hl)]
        if masked:
            onehot = jnp.where(lane == hl * _GATE_SLOTS + j, 1.0, 0.0).astype(BF16)
            s_buf[hl] = _dot_nt(jnp.concatenate([kj, onehot], axis=1), qaug_scr[hl])
        else:
            s_buf[hl] = _dot_nt(kj, qaug_scr[hl, :, 0:LANES])

    def softmax_pv(hl, j, s_buf, kind):
        s = s_buf[hl]
        cb = 0.0
        if kind == "own":
            s = s + bo_ref[hl]
        elif kind == "adj":
            s = s + ba_ref[hl]
        else:
            cb = far_ref[hl, :, 0:1]
        m_old = m_scr[hl]
        m_new = jnp.maximum(m_old, jnp.max(s, axis=0, keepdims=True) + cb)
        pr = jnp.exp(s - (m_new - cb)).astype(BF16)
        m_scr[hl] = m_new
        vt = vt_ref[0, hl * _VROWS:(hl + 1) * _VROWS, rows(j)]
        acc_scr[hl] = acc_scr[hl] * jnp.exp(m_old - m_new) + _dot(vt, pr)

    def stage(qk, sm):
        for hl in range(_MOBA_HG):
            if qk is not None:
                scores(hl, *qk)
            if sm is not None:
                softmax_pv(hl, *sm)

    n_far = jnp.maximum(c - 1, 0)
    n_pair = (n_far + 1) // 2

    @pl.when(c >= 1)
    def _():
        stage((0, sa_scr), None)

    def far_body(tt, carry):
        t0 = 2 * tt
        j1 = jnp.where(t0 + 1 < n_far, t0 + 1, c)
        stage((j1, sb_scr), (t0, sa_scr, "far"))
        stage((jnp.minimum(t0 + 2, c - 1), sa_scr), (j1, sb_scr, "far"))
        return carry

    lax.fori_loop(0, n_pair, far_body, 0)

    @pl.when(c >= 1)
    def _():
        stage((c, sb_scr, False), (c - 1, sa_scr, "adj"))

    @pl.when(c == 0)
    def _():
        stage((c, sb_scr, False), None)

    stage(None, (c, sb_scr, "own"))

    for pr_ in range(_MOBA_HG // 2):
        a0 = acc_scr[2 * pr_]
        a1 = acc_scr[2 * pr_ + 1]
        ot = jnp.concatenate([a0[:HEAD_DIM] / a0[HEAD_DIM:HEAD_DIM + 1],
                              a1[:HEAD_DIM] / a1[HEAD_DIM:HEAD_DIM + 1]], axis=0)
        o_ref[0, :, pr_ * LANES:(pr_ + 1) * LANES] = ot.T.astype(BF16)


def _moba(proj3, vbt, mask, bias_own_t, bias_adj_t, far):
    B, S, _ = proj3.shape
    L = MOBA_BLOCK
    hg = _MOBA_HG
    gw = hg * HEAD_DIM
    ng = MOBA_HEADS // hg
    return pl.pallas_call(
        _moba_kernel,
        out_shape=jax.ShapeDtypeStruct((B, S, MOBA_HEADS * HEAD_DIM), BF16),
        grid=(B, ng, S // L),
        in_specs=[
            pl.BlockSpec((1, L, gw), lambda b, g, c: (b, c, _QB * LANES // gw + g)),
            pl.BlockSpec((1, S, gw), lambda b, g, c: (b, 0, _KB * LANES // gw + g)),
            pl.BlockSpec((1, hg * _VROWS, S), lambda b, g, c: (b, g, 0)),
            pl.BlockSpec((1, L, LANES), lambda b, g, c: (b, c, g)),
            pl.BlockSpec((hg, L, L), lambda b, g, c: (g, 0, 0)),
            pl.BlockSpec((hg, L, L), lambda b, g, c: (g, 0, 0)),
            pl.BlockSpec((hg, 1, LANES), lambda b, g, c: (g, 0, 0)),
        ],
        out_specs=pl.BlockSpec((1, L, gw), lambda b, g, c: (b, c, g)),
        scratch_shapes=[pltpu.VMEM((hg, L, 2 * LANES), BF16), pltpu.VMEM((hg, L, L), F32),
                        pltpu.VMEM((hg, L, L), F32), pltpu.VMEM((hg, 1, L), F32),
                        pltpu.VMEM((hg, _VROWS, L), F32)],
        compiler_params=_params(("arbitrary", "arbitrary", "arbitrary")),
        name="moba_attn",
    )(proj3, proj3, vbt, mask, bias_own_t, bias_adj_t, far)


def _outproj_kernel(x_ref, oa_ref, ob_ref, wa_ref, wb_ref, g_ref, wrh_ref, wrl_ref, br_ref,
                    x1_ref, h2_ref, idx_ref, gate_ref, rank_ref, cnt_ref, run_scr):
    i = pl.program_id(0)

    @pl.when(i == 0)
    def _():
        run_scr[...] = jnp.zeros_like(run_scr)

    x1 = x_ref[...] + _dot(oa_ref[...], wa_ref[...]) + _dot(ob_ref[...], wb_ref[...])
    x1_ref[...] = x1
    ms = jnp.mean(x1 * x1, axis=-1, keepdims=True)
    h2 = x1 * lax.rsqrt(ms + NORM_EPS) * g_ref[...]
    h2_ref[...] = h2
    hh, hl = _split_bf16(h2)
    logits = (_dot(hh, wrh_ref[...]) + _dot(hl, wrh_ref[...]) + _dot(hh, wrl_ref[...])) + br_ref[...]

    tm = logits.shape[0]
    lane = lax.broadcasted_iota(I32, (tm, LANES), 1)
    vals, idxs = [], []
    for _k in range(TOP_K):
        mx = jnp.max(logits, axis=-1, keepdims=True)
        ix = jnp.min(jnp.where(logits == mx, lane, LANES), axis=-1, keepdims=True)
        vals.append(mx)
        idxs.append(ix)
        logits = jnp.where(lane == ix, -3e38, logits)
    es = [jnp.exp(v - vals[0]) for v in vals]
    den = es[0] + es[1] + es[2] + es[3]

    onehots = [lane == ix for ix in idxs]
    ohsum = jnp.zeros((tm, LANES), F32)
    for oh in onehots:
        ohsum = ohsum + jnp.where(oh, 1.0, 0.0)
    r_i = lax.broadcasted_iota(I32, (tm, tm), 0)
    c_i = lax.broadcasted_iota(I32, (tm, tm), 1)
    lower = jnp.where(r_i > c_i, 1.0, 0.0).astype(BF16)
    base = run_scr[0:1, :] + _dot(lower, ohsum.astype(BF16))

    idx_out = jnp.zeros((tm, LANES), I32)
    gate_out = jnp.zeros((tm, LANES), F32)
    rank_out = jnp.zeros((tm, LANES), F32)
    for k in range(TOP_K):
        rk = jnp.sum(jnp.where(onehots[k], base, 0.0), axis=-1, keepdims=True)
        idx_out = jnp.where(lane == k, idxs[k], idx_out)
        gate_out = jnp.where(lane == k, es[k] / den, gate_out)
        rank_out = jnp.where(lane == k, rk, rank_out)
    idx_ref[...] = idx_out
    gate_ref[...] = gate_out
    rank_ref[...] = rank_out.astype(I32)
    run = run_scr[0:1, :] + jnp.sum(ohsum, axis=0, keepdims=True)
    run_scr[...] = jnp.broadcast_to(run, run_scr.shape)
    cnt_ref[...] = jnp.broadcast_to(run, cnt_ref.shape).astype(I32)


def _outproj(x2, oa, ob, wa, wb, g, wr_hi, wr_lo, br, tm=256):
    T, D = x2.shape
    Ka = oa.shape[1]
    Kb = ob.shape[1]
    row = lambda i: (i, 0)
    fix = lambda i: (0, 0)
    return pl.pallas_call(
        _outproj_kernel,
        out_shape=(
            jax.ShapeDtypeStruct((T, D), F32),
            jax.ShapeDtypeStruct((T, D), F32),
            jax.ShapeDtypeStruct((T, LANES), I32),
            jax.ShapeDtypeStruct((T, LANES), F32),
            jax.ShapeDtypeStruct((T, LANES), I32),
            jax.ShapeDtypeStruct((8, LANES), I32),
        ),
        grid=(T // tm,),
        in_specs=[
            pl.BlockSpec((tm, D), row),
            pl.BlockSpec((tm, Ka), row),
            pl.BlockSpec((tm, Kb), row),
            pl.BlockSpec((Ka, D), fix),
            pl.BlockSpec((Kb, D), fix),
            pl.BlockSpec((1, D), fix),
            pl.BlockSpec((D, LANES), fix),
            pl.BlockSpec((D, LANES), fix),
            pl.BlockSpec((1, LANES), fix),
        ],
        out_specs=(
            pl.BlockSpec((tm, D), row),
            pl.BlockSpec((tm, D), row),
            pl.BlockSpec((tm, LANES), row),
            pl.BlockSpec((tm, LANES), row),
            pl.BlockSpec((tm, LANES), row),
            pl.BlockSpec((8, LANES), fix),
        ),
        scratch_shapes=[pltpu.VMEM((8, LANES), F32)],
        compiler_params=_params(("arbitrary",)),
        name="outproj_router",
    )(x2, oa, ob, wa, wb, g, wr_hi, wr_lo, br)


_PERM_W = 256


def _prep_gu_kernel(w_ref, perm_ref, o_ref):
    pm = perm_ref[...]
    F = o_ref.shape[2] // 2
    hw = _PERM_W // 2
    for g in range(w_ref.shape[2] // _PERM_W):
        wt = w_ref[0, :, g * _PERM_W:(g + 1) * _PERM_W].astype(BF16)
        d = _dot(wt, pm).astype(BF16)
        o_ref[0, :, g * hw:(g + 1) * hw] = d[:, :hw]
        o_ref[0, :, F + g * hw:F + (g + 1) * hw] = d[:, hw:]


def _prep_gu(w_gu, tk=512):
    E, K, N2 = w_gu.shape
    idx = np.arange(_PERM_W)
    src = np.where(idx < _PERM_W // 2, 2 * idx, 2 * (idx - _PERM_W // 2) + 1)
    pm = np.zeros((_PERM_W, _PERM_W), np.float32)
    pm[src, idx] = 1.0
    return pl.pallas_call(
        _prep_gu_kernel,
        out_shape=jax.ShapeDtypeStruct((E, K, N2), BF16),
        grid=(E, K // tk),
        in_specs=[pl.BlockSpec((1, tk, N2), lambda e, k: (e, k, 0)),
                  pl.BlockSpec((_PERM_W, _PERM_W), lambda e, k: (0, 0))],
        out_specs=pl.BlockSpec((1, tk, N2), lambda e, k: (e, k, 0)),
        compiler_params=_params(("arbitrary", "arbitrary")),
        name="moe_prep_gate_up",
    )(w_gu, jnp.asarray(pm, dtype=BF16))


def _issue_row_gather(idx_ref, src_hbm, dst, sem):
    def body(h, carry):
        for pri in range(2):
            r = 2 * h + pri
            pltpu.make_async_copy(src_hbm.at[pl.ds(idx_ref[0, 0, r], 1)], dst.at[pl.ds(r, 1)],
                                  sem).start(priority=pri)
        return carry

    lax.fori_loop(0, dst.shape[0] // 2, body, 0)


def _moe_gu_kernel(be_ref, tokc_ref, tokn_ref, h_hbm, w_ref, bg_ref, bl_ref, o_ref, xbuf, sem):
    i = pl.program_id(0)
    slot = i % 2
    bm = xbuf.shape[1]

    @pl.when(i == 0)
    def _():
        _issue_row_gather(tokc_ref, h_hbm, xbuf.at[0], sem.at[0])

    @pl.when(i + 1 < pl.num_programs(0))
    def _():
        _issue_row_gather(tokn_ref, h_hbm, xbuf.at[1 - slot], sem.at[1 - slot])

    pltpu.make_async_copy(h_hbm.at[pl.ds(0, bm)], xbuf.at[slot], sem.at[slot]).wait()
    x = xbuf[slot].astype(BF16)
    F = o_ref.shape[1]
    nh = 2
    fh = F // nh
    for h in range(nh):
        xg = _dot(x, w_ref[0, :, h * fh:(h + 1) * fh]) + bg_ref[0, :, h * fh:(h + 1) * fh]
        xl = _dot(x, w_ref[0, :, F + h * fh:F + (h + 1) * fh]) + bl_ref[0, :, h * fh:(h + 1) * fh]
        xg = jnp.minimum(xg, SWIGLU_LIMIT)
        xl = jnp.clip(xl, -SWIGLU_LIMIT, SWIGLU_LIMIT)
        act = xg * jax.nn.sigmoid(SWIGLU_ALPHA * xg) * (xl + 1.0)
        o_ref[:, h * fh:(h + 1) * fh] = act.astype(BF16)


def _moe_gu(block_expert, buf_tok, h2, w_gu_bf, b_gate, b_lin, bm=MOE_BLOCK):
    P = buf_tok.shape[0]
    T, D = h2.shape
    E, _, N2 = w_gu_bf.shape
    F = N2 // 2
    nb = P // bm
    tok3 = buf_tok.reshape(nb, 1, bm)
    return pl.pallas_call(
        _moe_gu_kernel,
        out_shape=jax.ShapeDtypeStruct((P, F), BF16),
        grid_spec=pltpu.PrefetchScalarGridSpec(
            num_scalar_prefetch=1,
            grid=(nb,),
            in_specs=[
                pl.BlockSpec((1, 1, bm), lambda i, be: (i, 0, 0), memory_space=pltpu.SMEM),
                pl.BlockSpec((1, 1, bm), lambda i, be: (jnp.minimum(i + 1, nb - 1), 0, 0),
                             memory_space=pltpu.SMEM),
                pl.BlockSpec(memory_space=pl.ANY),
                pl.BlockSpec((1, D, N2), lambda i, be: (be[i], 0, 0)),
                pl.BlockSpec((1, 1, F), lambda i, be: (be[i], 0, 0)),
                pl.BlockSpec((1, 1, F), lambda i, be: (be[i], 0, 0)),
            ],
            out_specs=pl.BlockSpec((bm, F), lambda i, be: (i, 0)),
            scratch_shapes=[pltpu.VMEM((2, bm, D), F32), pltpu.SemaphoreType.DMA((2,))],
        ),
        compiler_params=_params(("arbitrary",)),
        name="moe_gather_gate_up",
    )(block_expert, tok3, tok3, h2, w_gu_bf, b_gate, b_lin)


def _moe_down_kernel(be_ref, nv_ref, inv_ref, h_ref, w_ref, b_ref, yt_hbm, ybuf, sem, w_scr):
    i = pl.program_id(0)
    n = pl.num_programs(0)
    slot = i % 2
    bm = ybuf.shape[1]

    def drain(step, s):
        nv = nv_ref[step]

        @pl.when(nv == bm)
        def _():
            pltpu.make_async_copy(ybuf.at[s], yt_hbm.at[pl.ds(0, bm)], sem.at[s]).wait()

        @pl.when(nv < bm)
        def _():
            def body(r, carry):
                pltpu.make_async_copy(ybuf.at[s, pl.ds(0, 1)], yt_hbm.at[pl.ds(0, 1)], sem.at[s]).wait()
                return carry

            lax.fori_loop(0, nv, body, 0)

    @pl.when(i >= 2)
    def _():
        drain(i - 2, slot)

    prev = be_ref[jnp.maximum(i - 1, 0)]

    @pl.when(jnp.logical_or(i == 0, be_ref[i] != prev))
    def _():
        w_scr[...] = w_ref[0].astype(BF16)

    ybuf[slot] = _dot(h_ref[...], w_scr[...]) + b_ref[0]

    def row_copy(r):
        return pltpu.make_async_copy(ybuf.at[slot, pl.ds(r, 1)], yt_hbm.at[pl.ds(inv_ref[0, 0, r], 1)],
                                     sem.at[slot])

    def issue(r, carry):
        row_copy(r).start()
        return carry

    def issue_pair(h, carry):
        for pri in range(2):
            row_copy(2 * h + pri).start(priority=pri)
        return carry

    @pl.when(nv_ref[i] == bm)
    def _():
        lax.fori_loop(0, bm // 2, issue_pair, 0)

    @pl.when(nv_ref[i] < bm)
    def _():
        lax.fori_loop(0, nv_ref[i], issue, 0)

    @pl.when(i == n - 1)
    def _():
        @pl.when(n >= 2)
        def _():
            drain(i - 1, 1 - slot)
        drain(i, slot)


def _moe_down(block_expert, n_valid, inv, hact, w_down, b_down, n_rows, bm=MOE_BLOCK):
    P, F = hact.shape
    E, _, D = w_down.shape
    nb = P // bm
    return pl.pallas_call(
        _moe_down_kernel,
        out_shape=jax.ShapeDtypeStruct((n_rows, D), F32),
        grid_spec=pltpu.PrefetchScalarGridSpec(
            num_scalar_prefetch=2,
            grid=(nb,),
            in_specs=[
                pl.BlockSpec((1, 1, bm), lambda i, be, nv: (i, 0, 0), memory_space=pltpu.SMEM),
                pl.BlockSpec((bm, F), lambda i, be, nv: (i, 0)),
                pl.BlockSpec((1, F, D), lambda i, be, nv: (be[i], 0, 0)),
                pl.BlockSpec((1, 1, D), lambda i, be, nv: (be[i], 0, 0)),
            ],
            out_specs=pl.BlockSpec(memory_space=pl.ANY),
            scratch_shapes=[pltpu.VMEM((2, bm, D), F32), pltpu.SemaphoreType.DMA((2,)),
                            pltpu.VMEM((F, D), BF16)],
        ),
        compiler_params=_params(("arbitrary",)),
        name="moe_down_scatter",
    )(block_expert, n_valid, inv.reshape(nb, 1, bm), hact, w_down, b_down)


def _combine_kernel(x1_ref, gate_ref, y_ref, o_ref):
    tb = x1_ref.shape[0]
    acc = x1_ref[...]
    gates = gate_ref[...]
    for k in range(TOP_K):
        acc = acc + y_ref[k * tb:(k + 1) * tb, :] * gates[:, k:k + 1]
    o_ref[...] = acc


def _combine(x1, gates, yt, tb):
    T, D = x1.shape
    return pl.pallas_call(
        _combine_kernel,
        out_shape=jax.ShapeDtypeStruct((T, D), F32),
        grid=(T // tb,),
        in_specs=[
            pl.BlockSpec((tb, D), lambda i: (i, 0)),
            pl.BlockSpec((tb, LANES), lambda i: (i, 0)),
            pl.BlockSpec((TOP_K * tb, D), lambda i: (i, 0)),
        ],
        out_specs=pl.BlockSpec((tb, D), lambda i: (i, 0)),
        compiler_params=_params(("arbitrary",)),
        name="moe_combine",
    )(x1, gates, yt)


def _t5_bucket(dist):
    max_exact = T5_BUCKETS // 2
    d = jnp.maximum(dist, 0)
    df = jnp.maximum(d, 1).astype(F32)
    large = max_exact + (jnp.log(df / max_exact) / math.log(T5_MAX_DISTANCE / max_exact)
                         * (T5_BUCKETS - max_exact)).astype(I32)
    large = jnp.minimum(large, T5_BUCKETS - 1)
    return jnp.where(d < max_exact, d, large)


def _toeplitz_bias(rel_t, R, C, offset, lo, hi):
    H = rel_t.shape[0]
    d = np.arange(R + C - 1) - (C - 1) + offset
    w = jnp.where(jnp.asarray((d >= lo) & (d < hi))[None], rel_t[:, _t5_bucket(jnp.asarray(d, I32))], NEG)
    m = R + C
    w_ext = jnp.pad(w[:, ::-1], ((0, 0), (0, 1)))
    y = jnp.tile(w_ext, (1, R))[:, :R * (m - 1)].reshape(H, R, m - 1)
    return y[:, :, R - 1:R - 1 + C].astype(F32)


def _pair_major(a, n_pair, n_half, n_c):
    sh = a.shape
    a = a.reshape((n_pair, n_half, n_c) + sh[1:])
    a = jnp.swapaxes(a, 1, 2)
    return a.reshape(sh)


def kernel(x, attn_norm_g, w_in, swa_q_gain, swa_k_gain, swa_sinks, moba_q_gain, moba_k_gain, rel_bias,
           w_out, ffn_norm_g, w_router, b_router, w_gate_up, b_gate_up, w_down, b_down):
    B, S, D = x.shape
    T = B * S
    assert w_in.shape[0] == 1, "single-layer kernel"
    qa_w = SWA_Q_HEADS * HEAD_DIM
    kv_w = SWA_KV_HEADS * HEAD_DIM
    mb_w = MOBA_HEADS * HEAD_DIM
    G = SWA_Q_HEADS // SWA_KV_HEADS
    npair = SWA_KV_HEADS // 2

    w0 = w_in[0]
    wq = w0[:, :qa_w].reshape(D, SWA_Q_HEADS, HEAD_DIM)
    wq = jnp.swapaxes(wq.reshape(D, npair, 2, G, HEAD_DIM), 2, 3).reshape(D, qa_w)
    w_in_bf = jnp.concatenate([wq, w0[:, qa_w:]], axis=1).astype(BF16)
    ones = jnp.ones((HEAD_DIM,), F32)
    tile = lambda v, n: jnp.tile(v.astype(F32), n)
    gain_col = jnp.concatenate([
        tile(swa_q_gain[0] * ATTN_SCALE, SWA_Q_HEADS), tile(swa_k_gain[0], SWA_KV_HEADS),
        tile(ones, SWA_KV_HEADS), tile(moba_q_gain[0] * ATTN_SCALE, MOBA_HEADS),
        tile(moba_k_gain[0], MOBA_HEADS), tile(ones, MOBA_HEADS)])[None, :]
    flag_np = np.concatenate([np.ones(qa_w + kv_w), np.zeros(kv_w), np.ones(2 * mb_w), np.zeros(mb_w)])
    flag_col = jnp.asarray(flag_np[None, :], F32)

    wo = w_out[0]
    wa = wo[:qa_w].reshape(npair, 2, G, HEAD_DIM, D)
    wa = jnp.swapaxes(wa, 1, 2).reshape(qa_w, D).astype(BF16)
    wb = wo[qa_w:].astype(BF16)

    rel_a = rel_bias[:, :SWA_Q_HEADS].T.astype(F32)
    rel_b = rel_bias[:, SWA_Q_HEADS:].T.astype(F32)
    W = SWA_WINDOW
    bias_a = _toeplitz_bias(rel_a, W, 2 * W, W, 0, W)
    bias_swa = _pair_major(bias_a, npair, 2, G).reshape(npair, G, 2 * W, 2 * W)
    sink_col = jnp.broadcast_to(swa_sinks[0].astype(F32)[:, None, None], (SWA_Q_HEADS, W, 1))
    sink_swa = _pair_major(sink_col, npair, 2, G).reshape(npair, G, 2 * W, 1)
    L = MOBA_BLOCK
    bias_own_t = jnp.swapaxes(_toeplitz_bias(rel_b, L, L, 0, 0, 2 * L), 1, 2)
    bias_adj_t = jnp.swapaxes(_toeplitz_bias(rel_b, L, L, L, 0, 2 * L), 1, 2)
    far = jnp.broadcast_to(rel_b[:, T5_BUCKETS - 1][:, None, None], (MOBA_HEADS, 1, LANES))

    x2 = x.reshape(T, D)
    proj = _inproj(x2, attn_norm_g[0][None, :].astype(F32), w_in_bf, gain_col, flag_col)
    proj3 = proj.reshape(B, S, proj.shape[1])
    kmean = _kmean(proj3)
    nblk = S // L
    assert nblk <= _GATE_SLOTS, "selection mask packs at most 16 key blocks per head"
    km = kmean.reshape(B, nblk, MOBA_HEADS, HEAD_DIM).transpose(0, 2, 3, 1)
    km = jnp.pad(km, ((0, 0), (0, 0), (0, 0), (0, _GATE_SLOTS - nblk)))
    eye = jnp.eye(MOBA_HEADS, dtype=F32)
    kmbd = (km[:, :, :, None, :] * eye[None, :, None, :, None]).reshape(B, mb_w, MOBA_HEADS * _GATE_SLOTS)
    kmbd_hi, kmbd_lo = _split_bf16(kmbd)
    mask = _select(proj3, kmbd_hi, kmbd_lo)
    oa = _swa(proj3, bias_swa, sink_swa)
    vbt = jnp.swapaxes(proj3[:, :, _VB * LANES:], 1, 2).reshape(B, MOBA_HEADS, HEAD_DIM, S)
    vbt = jnp.concatenate([vbt, jnp.ones((B, MOBA_HEADS, 1, S), BF16),
                           jnp.zeros((B, MOBA_HEADS, _VROWS - HEAD_DIM - 1, S), BF16)],
                          axis=2).reshape(B, MOBA_HEADS * _VROWS, S)
    ob = _moba(proj3, vbt, mask, bias_own_t, bias_adj_t, far)

    wr = jnp.pad(w_router[0].astype(F32), ((0, 0), (0, LANES - N_EXPERTS)))
    wr_hi, wr_lo = _split_bf16(wr)
    br = jnp.pad(b_router[0].astype(F32), (0, LANES - N_EXPERTS), constant_values=NEG)[None, :]
    x1, h2, idx_o, gate_o, rank_o, cnt_o = _outproj(
        x2, oa.reshape(T, qa_w), ob.reshape(T, mb_w), wa, wb, ffn_norm_g[0][None, :].astype(F32),
        wr_hi, wr_lo, br)

    A = T * TOP_K
    counts = cnt_o[0, :N_EXPERTS]
    padded = (counts + MOE_BLOCK - 1) // MOE_BLOCK * MOE_BLOCK
    pad_ends = jnp.cumsum(padded)
    pad_starts = pad_ends - padded
    n_blocks = -(-(A + N_EXPERTS * (MOE_BLOCK - 1)) // MOE_BLOCK)
    P = n_blocks * MOE_BLOCK
    top_idx = idx_o[:, :TOP_K]
    is_e = top_idx[:, :, None] == jnp.arange(N_EXPERTS, dtype=I32)
    dest = jnp.sum(jnp.where(is_e, pad_starts, 0), axis=-1) + rank_o[:, :TOP_K]
    tb = _COMBINE_TB
    t_idx = jnp.arange(T, dtype=I32)[:, None]
    out_row = (t_idx // tb) * (TOP_K * tb) + jnp.arange(TOP_K, dtype=I32)[None, :] * tb + t_idx % tb
    inv = jnp.full((P,), -1, I32).at[dest.reshape(A)].set(out_row.reshape(A))
    inv_c = jnp.maximum(inv, 0)
    buf_tok = jnp.where(inv >= 0, (inv_c // (TOP_K * tb)) * tb + inv_c % tb, 0)
    blk_start = jnp.arange(n_blocks, dtype=I32)[:, None] * MOE_BLOCK
    block_expert = jnp.minimum(jnp.sum((pad_ends[None, :] <= blk_start).astype(I32), axis=1),
                               N_EXPERTS - 1).astype(I32)
    is_be = block_expert[:, None] == jnp.arange(N_EXPERTS, dtype=I32)[None, :]
    used = blk_start[:, 0] - jnp.sum(jnp.where(is_be, pad_starts[None, :], 0), axis=1)
    n_valid = jnp.clip(jnp.sum(jnp.where(is_be, counts[None, :], 0), axis=1) - used, 0, MOE_BLOCK).astype(I32)

    bgu = b_gate_up[0].astype(F32).reshape(N_EXPERTS, -1, 2)
    b_gate = bgu[:, :, 0][:, None, :]
    b_lin = bgu[:, :, 1][:, None, :]
    w_gu_bf = _prep_gu(w_gate_up[0])
    hact = _moe_gu(block_expert, buf_tok, h2, w_gu_bf, b_gate, b_lin)
    yt = _moe_down(block_expert, n_valid, inv_c, hact, w_down[0], b_down[0].astype(F32)[:, None, :], A)
    out = _combine(x1, gate_o, yt, tb)
    return out.reshape(B, S, D)
```
